```python
import math
import jax, jax.numpy as jnp
from jax import lax
import numpy as np

D_MODEL = 4096
BATCH = 2
SEQ = 4096
DEPTH = 4

SB_HEADS = 8
SB_HEAD_DIM = 128
SB_WIDTH = SB_HEADS * SB_HEAD_DIM
MLA_HEADS = 8
MLA_Q_LORA = 1024
MLA_KV_LORA = 512
MLA_NOPE_DIM = 128
MLA_ROPE_DIM = 64
MLA_V_DIM = 128
MLA_QK_DIM = MLA_NOPE_DIM + MLA_ROPE_DIM
MLA_WIDTH = MLA_HEADS * MLA_V_DIM
ROPE_THETA = 10000.0
HG_HEADS = 8
HG_KEY_DIM = 128
HG_VAL_DIM = 128
HG_WIDTH = HG_HEADS * HG_KEY_DIM
HG_CHUNK = 64
N_BRANCHES = 3
D_FF = 2 * D_MODEL
CONV_WIDTH = 3
Q_BLOCK = 128
EPS = 1e-6
MASK_NEG = -1e30
LB_FLOOR = 1e-30

SPLIT_SIZES = (SB_WIDTH, SB_WIDTH, SB_WIDTH,
               MLA_Q_LORA, MLA_KV_LORA, MLA_ROPE_DIM,
               HG_WIDTH, HG_WIDTH, HG_WIDTH, HG_WIDTH,
               N_BRANCHES * D_MODEL)
IN_COLS = 3 * SB_WIDTH + MLA_Q_LORA + MLA_KV_LORA + MLA_ROPE_DIM + 4 * HG_WIDTH + N_BRANCHES * D_MODEL

kernel_name = "hybrid_sb_mla_hgrn2_gated_merge"


def split_points():
    pts, acc = [], 0
    for s in SPLIT_SIZES[:-1]:
        acc += s
        pts.append(acc)
    return pts


def rmsnorm(x, g):
    xf = x.astype(jnp.float32)
    y = xf * lax.rsqrt(jnp.mean(xf * xf, axis=-1, keepdims=True) + EPS)
    return (y * g.astype(jnp.float32)).astype(x.dtype)


def rope_tables(seq):
    pos = jnp.arange(seq, dtype=jnp.float32)
    inv = ROPE_THETA ** (-jnp.arange(0, MLA_ROPE_DIM, 2, dtype=jnp.float32) / MLA_ROPE_DIM)
    ang = pos[:, None] * inv[None, :]
    return jnp.cos(ang), jnp.sin(ang)


def apply_rope(x, cos, sin):
    xf = x.astype(jnp.float32)
    x1, x2 = jnp.split(xf, 2, axis=-1)
    c = cos[None, :, None, :]
    s = sin[None, :, None, :]
    return jnp.concatenate([x1 * c - x2 * s, x2 * c + x1 * s], axis=-1).astype(x.dtype)


def query_block_sweep(block_fn, q):
    B, S, H, D = q.shape
    nb = S // Q_BLOCK
    q_blocks = q.reshape(B, nb, Q_BLOCK, H, D).transpose(1, 0, 2, 3, 4)
    starts = jnp.arange(nb, dtype=jnp.int32) * Q_BLOCK
    out = lax.map(lambda a: block_fn(a[0], a[1] + jnp.arange(Q_BLOCK, dtype=jnp.int32)),
                  (q_blocks, starts))
    return out.transpose(1, 0, 2, 3, 4).reshape(B, S, H, out.shape[-1])


def stick_breaking_attention(q, k, v):
    S, Dh = q.shape[1], q.shape[3]
    scale = Dh ** -0.5
    key_pos = jnp.arange(S, dtype=jnp.int32)

    def block(q_blk, q_pos):
        z = jnp.einsum('bqhd,bkhd->bhqk', q_blk, k).astype(jnp.float32) * scale
        strict = key_pos[None, :] < q_pos[:, None]
        log_1m_beta = jnp.where(strict, jax.nn.log_sigmoid(-z), 0.0)
        suffix = lax.cumsum(log_1m_beta, axis=3, reverse=True) - log_1m_beta
        log_w = jnp.where(strict, jax.nn.log_sigmoid(z) + suffix, MASK_NEG)
        w = jnp.exp(log_w)
        return jnp.einsum('bhqk,bkhd->bqhd', w.astype(v.dtype), v)

    return query_block_sweep(block, q)


def causal_softmax_attention(q, k, v):
    S, Dqk = q.shape[1], q.shape[3]
    scale = Dqk ** -0.5
    key_pos = jnp.arange(S, dtype=jnp.int32)

    def block(q_blk, q_pos):
        z = jnp.einsum('bqhd,bkhd->bhqk', q_blk, k).astype(jnp.float32) * scale
        causal = key_pos[None, :] <= q_pos[:, None]
        p = jax.nn.softmax(jnp.where(causal, z, MASK_NEG), axis=-1)
        return jnp.einsum('bhqk,bkhd->bqhd', p.astype(v.dtype), v)

    return query_block_sweep(block, q)


def hgrn2_chunkwise(q, k, v, log_f):
    B, S, H, Dk = q.shape
    Dv = v.shape[-1]
    nc = S // HG_CHUNK

    def to_chunks(t):
        return t.reshape(B, nc, HG_CHUNK, H, t.shape[-1]).transpose(1, 0, 3, 2, 4)

    causal = (jnp.arange(HG_CHUNK)[:, None] >= jnp.arange(HG_CHUNK)[None, :])[:, :, None]

    def step(state, inp):
        q_c, k_c, v_c, g_c = inp
        b = jnp.cumsum(g_c, axis=2)
        o_inter = jnp.einsum('bhtd,bhde->bhte', q_c * jnp.exp(b), state)
        diff = b[:, :, :, None, :] - b[:, :, None, :, :]
        decay = jnp.exp(jnp.where(causal, diff, MASK_NEG))
        attn = jnp.einsum('bhtd,bhsd,bhtsd->bhts', q_c, k_c, decay)
        o_intra = jnp.einsum('bhts,bhse->bhte', attn, v_c)
        b_last = b[:, :, -1:, :]
        k_dec = k_c * jnp.exp(b_last - b)
        new_state = state * jnp.exp(b_last)[:, :, 0, :, None] + jnp.einsum('bhsd,bhse->bhde', k_dec, v_c)
        return new_state, o_inter + o_intra

    state0 = jnp.zeros((B, H, Dk, Dv), jnp.float32)
    _, out = lax.scan(step, state0, (to_chunks(q), to_chunks(k), to_chunks(v), to_chunks(log_f)))
    return out.transpose(1, 0, 3, 2, 4).reshape(B, S, H, Dv)


def stick_breaking_branch(q, k, v, q_norm, k_norm):
    B, S, _ = q.shape
    shp = (B, S, SB_HEADS, SB_HEAD_DIM)
    qh = rmsnorm(q.reshape(shp), q_norm)
    kh = rmsnorm(k.reshape(shp), k_norm)
    o = stick_breaking_attention(qh, kh, v.reshape(shp))
    return o.reshape(B, S, SB_WIDTH)


def mla_branch(cq, ckv, kr, cos, sin, q_a_norm, kv_a_norm, w_q_b, w_kv_b, q_norm, k_norm):
    B, S, _ = cq.shape
    q = jnp.einsum('bsr,rhd->bshd', rmsnorm(cq, q_a_norm), w_q_b)
    q = rmsnorm(q, q_norm)
    q = jnp.concatenate([q[..., :MLA_NOPE_DIM], apply_rope(q[..., MLA_NOPE_DIM:], cos, sin)], axis=-1)
    kv = jnp.einsum('bsr,rhd->bshd', rmsnorm(ckv, kv_a_norm), w_kv_b)
    k_nope, v = kv[..., :MLA_NOPE_DIM], kv[..., MLA_NOPE_DIM:]
    k_rope = jnp.broadcast_to(kr[:, :, None, :], (B, S, MLA_HEADS, MLA_ROPE_DIM))
    k = rmsnorm(jnp.concatenate([k_nope, k_rope], axis=-1), k_norm)
    k = jnp.concatenate([k[..., :MLA_NOPE_DIM], apply_rope(k[..., MLA_NOPE_DIM:], cos, sin)], axis=-1)
    o = causal_softmax_attention(q, k, v)
    return o.reshape(B, S, MLA_WIDTH)


def hgrn2_branch(q, f_logits, i, g, lb, out_norm):
    B, S, _ = q.shape
    dtype = q.dtype
    shp = (B, S, HG_HEADS, HG_KEY_DIM)
    lb = lb.reshape(HG_HEADS, HG_KEY_DIM)
    z = f_logits.astype(jnp.float32).reshape(shp)
    log_f = jnp.logaddexp(jnp.log(jnp.maximum(lb, LB_FLOOR)), jnp.log1p(-lb) + jax.nn.log_sigmoid(z))
    k = -jnp.expm1(log_f)
    o = hgrn2_chunkwise(q.astype(jnp.float32).reshape(shp), k,
                        i.astype(jnp.float32).reshape(B, S, HG_HEADS, HG_VAL_DIM), log_f)
    o = rmsnorm(o, out_norm) * jax.nn.silu(g.astype(jnp.float32).reshape(B, S, HG_HEADS, HG_VAL_DIM))
    return o.reshape(B, S, HG_WIDTH).astype(dtype)


def conv_glu_ffn(h, w_up, conv_w, w_down):
    S = h.shape[1]
    u = h @ w_up
    up = jnp.pad(u, ((0, 0), (CONV_WIDTH - 1, 0), (0, 0)))
    u = sum(up[:, j:j + S, :] * conv_w[j] for j in range(CONV_WIDTH))
    gate, val = jnp.split(u, 2, axis=-1)
    return (jax.nn.silu(gate) * val) @ w_down


def setup_inputs(seed: int = 0) -> dict:
    key = jax.random.key(seed)
    ks = jax.random.split(key, 24)
    L = DEPTH

    def w(k, shape, fan_in):
        return jax.random.normal(k, shape, jnp.float32) * fan_in ** -0.5

    def gain(k, shape):
        return 1.0 + 0.02 * jax.random.normal(k, shape, jnp.float32)

    return {
        "x": jax.random.normal(ks[0], (BATCH, SEQ, D_MODEL), jnp.float32),
        "lower_bounds": 0.5 * jax.random.normal(ks[1], (L, HG_WIDTH), jnp.float32),
        "attn_norm": gain(ks[2], (L, D_MODEL)),
        "w_in": w(ks[3], (L, D_MODEL, IN_COLS), D_MODEL),
        "sb_q_norm": gain(ks[4], (L, SB_HEAD_DIM)),
        "sb_k_norm": gain(ks[5], (L, SB_HEAD_DIM)),
        "mla_q_a_norm": gain(ks[6], (L, MLA_Q_LORA)),
        "mla_kv_a_norm": gain(ks[7], (L, MLA_KV_LORA)),
        "mla_w_q_b": w(ks[8], (L, MLA_Q_LORA, MLA_HEADS, MLA_QK_DIM), MLA_Q_LORA),
        "mla_w_kv_b": w(ks[9], (L, MLA_KV_LORA, MLA_HEADS, MLA_NOPE_DIM + MLA_V_DIM), MLA_KV_LORA),
        "mla_q_norm": gain(ks[10], (L, MLA_QK_DIM)),
        "mla_k_norm": gain(ks[11], (L, MLA_QK_DIM)),
        "hg_out_norm": gain(ks[12], (L, HG_VAL_DIM)),
        "w_branch_sb": w(ks[13], (L, SB_WIDTH, D_MODEL), SB_WIDTH),
        "w_branch_mla": w(ks[14], (L, MLA_WIDTH, D_MODEL), MLA_WIDTH),
        "w_branch_hg": w(ks[15], (L, HG_WIDTH, D_MODEL), HG_WIDTH),
        "w_out": w(ks[16], (L, D_MODEL, D_MODEL), D_MODEL),
        "ffn_norm": gain(ks[17], (L, D_MODEL)),
        "w_up": w(ks[18], (L, D_MODEL, 2 * D_FF), D_MODEL),
        "ffn_conv": w(ks[19], (L, CONV_WIDTH, 2 * D_FF), CONV_WIDTH),
        "w_down": w(ks[20], (L, D_FF, D_MODEL), D_FF),
    }


def reference(x, lower_bounds, attn_norm, w_in, sb_q_norm, sb_k_norm, mla_q_a_norm, mla_kv_a_norm,
              mla_w_q_b, mla_w_kv_b, mla_q_norm, mla_k_norm, hg_out_norm, w_branch_sb, w_branch_mla,
              w_branch_hg, w_out, ffn_norm, w_up, ffn_conv, w_down):
    B, S, _ = x.shape
    p = jax.nn.softmax(lower_bounds.astype(jnp.float32), axis=0)
    lbs = jnp.cumsum(p, axis=0) - p[0:1]
    cos, sin = rope_tables(S)
    pts = split_points()
    for l in range(DEPTH):
        h = rmsnorm(x, attn_norm[l])
        proj = h @ w_in[l]
        (sb_q, sb_k, sb_v, mla_cq, mla_ckv, mla_kr,
         hg_q, hg_f, hg_i, hg_g, gate_logits) = jnp.split(proj, pts, axis=-1)
        o_sb = stick_breaking_branch(sb_q, sb_k, sb_v, sb_q_norm[l], sb_k_norm[l])
        o_mla = mla_branch(mla_cq, mla_ckv, mla_kr, cos, sin, mla_q_a_norm[l], mla_kv_a_norm[l],
                           mla_w_q_b[l], mla_w_kv_b[l], mla_q_norm[l], mla_k_norm[l])
        o_hg = hgrn2_branch(hg_q, hg_f, hg_i, hg_g, lbs[l], hg_out_norm[l])
        gates = jax.nn.sigmoid(gate_logits.astype(jnp.float32)).astype(x.dtype)
        gates = gates.reshape(B, S, N_BRANCHES, D_MODEL)
        merged = (gates[:, :, 0] * (o_sb @ w_branch_sb[l])
                  + gates[:, :, 1] * (o_mla @ w_branch_mla[l])
                  + gates[:, :, 2] * (o_hg @ w_branch_hg[l]))
        x = x + merged @ w_out[l]
        x = x + conv_glu_ffn(rmsnorm(x, ffn_norm[l]), w_up[l], ffn_conv[l], w_down[l])
    return x
```

```python
import functools

import jax
import jax.numpy as jnp
from jax import lax
from jax.experimental import pallas as pl
from jax.experimental.pallas import tpu as pltpu

F32 = jnp.float32
BF16 = jnp.bfloat16

HEAD_DIM = 128
N_HEADS = 8
MLA_ROPE = 64
MLA_QK = HEAD_DIM + MLA_ROPE
MLA_PAD = 256
HG_CHUNK = 64
ROPE_THETA = 10000.0
EPS = 1e-6
MASK_NEG = -1e30
LB_FLOOR = 1e-30

VMEM_LIMIT_BYTES = 56 * 1024 * 1024


def _params(*sem):
    return pltpu.CompilerParams(dimension_semantics=sem, vmem_limit_bytes=VMEM_LIMIT_BYTES)


def _dot(a, b):
    return jnp.dot(a, b, preferred_element_type=F32)


def _dot_nt(a, b):
    return lax.dot_general(a, b, (((1,), (1,)), ((), ())), preferred_element_type=F32)


def _dot_tn(a, b):
    return lax.dot_general(a, b, (((0,), (0,)), ((), ())), preferred_element_type=F32)


def _rms(x, width):
    return x * lax.rsqrt(jnp.sum(x * x, axis=-1, keepdims=True) / width + EPS)


def _rmsnorm_kernel(x_ref, g_ref, o_ref):
    x = x_ref[...]
    o_ref[...] = (_rms(x, x.shape[-1]) * g_ref[...]).astype(o_ref.dtype)


def rmsnorm_rows(x, g, tm=256):
    m, d = x.shape
    return pl.pallas_call(
        _rmsnorm_kernel,
        grid=(m // tm,),
        in_specs=[pl.BlockSpec((tm, d), lambda i: (i, 0)),
                  pl.BlockSpec((1, d), lambda i: (0, 0))],
        out_specs=pl.BlockSpec((tm, d), lambda i: (i, 0)),
        out_shape=jax.ShapeDtypeStruct((m, d), BF16),
        compiler_params=_params("parallel"),
        name="rmsnorm_rows",
    )(x, g.reshape(1, d))


def _mm_kernel(a_ref, b_ref, o_ref):
    o_ref[...] = _dot(a_ref[...], b_ref[...]).astype(o_ref.dtype)


def matmul(a, b, out_dtype, tm, tn):
    m, k = a.shape
    n = b.shape[1]
    return pl.pallas_call(
        _mm_kernel,
        grid=(m // tm, n // tn),
        in_specs=[pl.BlockSpec((tm, k), lambda i, j: (i, 0)),
                  pl.BlockSpec((k, tn), lambda i, j: (0, j))],
        out_specs=pl.BlockSpec((tm, tn), lambda i, j: (i, j)),
        out_shape=jax.ShapeDtypeStruct((m, n), out_dtype),
        compiler_params=_params("parallel", "arbitrary"),
        name="matmul",
    )(a, b)


def _mm_res_kernel(a_ref, b_ref, r_ref, o_ref):
    d = _dot(a_ref[...], b_ref[...])

    @pl.when(pl.program_id(2) == 0)
    def _():
        o_ref[...] = r_ref[...] + d

    @pl.when(pl.program_id(2) != 0)
    def _():
        o_ref[...] += d


def matmul_residual(a, b, res, tm, tn, tk):
    m, k = a.shape
    n = b.shape[1]
    return pl.pallas_call(
        _mm_res_kernel,
        grid=(m // tm, n // tn, k // tk),
        in_specs=[pl.BlockSpec((tm, tk), lambda i, j, kk: (i, kk)),
                  pl.BlockSpec((tk, tn), lambda i, j, kk: (kk, j)),
                  pl.BlockSpec((tm, tn), lambda i, j, kk: (i, j))],
        out_specs=pl.BlockSpec((tm, tn), lambda i, j, kk: (i, j)),
        out_shape=jax.ShapeDtypeStruct((m, n), F32),
        compiler_params=_params("parallel", "arbitrary", "arbitrary"),
        name="matmul_residual",
    )(a, b, res)


def _headnorm_kernel(x_ref, g_ref, o_ref):
    g = g_ref[0]
    for h in range(x_ref.shape[1] // HEAD_DIM):
        sl = slice(h * HEAD_DIM, (h + 1) * HEAD_DIM)
        x = x_ref[:, sl].astype(F32)
        o_ref[:, sl] = (_rms(x, HEAD_DIM) * g).astype(o_ref.dtype)


def headnorm_qk(proj, gains, width, tm=512):
    m = proj.shape[0]
    return pl.pallas_call(
        _headnorm_kernel,
        grid=(m // tm, 2),
        in_specs=[pl.BlockSpec((tm, width), lambda i, j: (i, j)),
                  pl.BlockSpec((1, 1, HEAD_DIM), lambda i, j: (j, 0, 0))],
        out_specs=pl.BlockSpec((tm, width), lambda i, j: (i, j)),
        out_shape=jax.ShapeDtypeStruct((m, 2 * width), BF16),
        compiler_params=_params("parallel", "arbitrary"),
        name="headnorm_qk",
    )(proj, gains)


def _sb_attn_kernel(q_ref, k_ref, v_ref, o_ref, *, blk, scale):
    qi = pl.program_id(2)
    q = q_ref[...]
    row = lax.broadcasted_iota(jnp.int32, (blk, blk), 0)
    col = lax.broadcasted_iota(jnp.int32, (blk, blk), 1)
    strict = col < row
    later = (row > col).astype(BF16)

    def step(j, carry, acc, masked):
        ks = k_ref[pl.ds(pl.multiple_of(j * blk, blk), blk), :]
        vs = v_ref[pl.ds(pl.multiple_of(j * blk, blk), blk), :]
        z = _dot_nt(q, ks) * scale
        log_1m_beta = -(jnp.maximum(z, 0.0) + jnp.log1p(jnp.exp(-jnp.abs(z))))
        log_beta = z + log_1m_beta
        if masked:
            log_1m_beta = jnp.where(strict, log_1m_beta, 0.0)
        hi = log_1m_beta.astype(BF16)
        lo = (log_1m_beta - hi.astype(F32)).astype(BF16)
        suffix = _dot(hi, later) + _dot(lo, later)
        w = jnp.exp(log_beta + suffix + carry)
        if masked:
            w = jnp.where(strict, w, 0.0)
        acc = acc + _dot(w.astype(BF16), vs)
        carry = carry + jnp.sum(log_1m_beta, axis=1, keepdims=True)
        return carry, acc

    carry, acc = step(qi, jnp.zeros((blk, 1), F32), jnp.zeros((blk, HEAD_DIM), F32), True)

    def body(t, c):
        return step(qi - 1 - t, c[0], c[1], False)

    carry, acc = lax.fori_loop(0, qi, body, (carry, acc))
    o_ref[...] = acc.astype(o_ref.dtype)


def sb_attention(qk, proj, batch, seq, blk=256):
    width = N_HEADS * HEAD_DIM
    nq = seq // blk
    kern = functools.partial(_sb_attn_kernel, blk=blk, scale=HEAD_DIM ** -0.5)
    return pl.pallas_call(
        kern,
        grid=(batch, N_HEADS, nq),
        in_specs=[pl.BlockSpec((blk, HEAD_DIM), lambda b, h, i: (b * nq + i, h)),
                  pl.BlockSpec((seq, HEAD_DIM), lambda b, h, i: (b, N_HEADS + h)),
                  pl.BlockSpec((seq, HEAD_DIM), lambda b, h, i: (b, 2 * N_HEADS + h))],
        out_specs=pl.BlockSpec((blk, HEAD_DIM), lambda b, h, i: (b * nq + i, h)),
        out_shape=jax.ShapeDtypeStruct((batch * seq, width), BF16),
        compiler_params=_params("parallel", "parallel", "arbitrary"),
        name="sb_attention",
    )(qk, qk, proj)


def _rope_lanes(x, cos_t, sin_t):
    lane = lax.broadcasted_iota(jnp.int32, x.shape, 1)
    half = MLA_ROPE // 2
    partner = jnp.where(lane < half, pltpu.roll(x, HEAD_DIM - half, 1), pltpu.roll(x, half, 1))
    return x * cos_t + partner * sin_t


def _mla_prep_kernel(p_ref, kr_ref, qa_ref, kva_ref, wq_ref, wkn_ref, wv_ref, gq_ref, gk_ref,
                     cos_ref, sin_ref, q_out, k_out, v_out, *, q_lora):
    cos_t = cos_ref[...]
    sin_t = sin_ref[...]
    gq = gq_ref[...]
    gk = gk_ref[...]
    cq = p_ref[:, :q_lora].astype(F32)
    cqn = (_rms(cq, q_lora) * qa_ref[...]).astype(BF16)
    qraw = _dot(cqn, wq_ref[...])
    for h in range(N_HEADS):
        blk = qraw[:, h * MLA_PAD:(h + 1) * MLA_PAD]
        y = _rms(blk, MLA_QK) * gq
        q_out[:, h * MLA_PAD:h * MLA_PAD + HEAD_DIM] = y[:, :HEAD_DIM].astype(q_out.dtype)
        q_out[:, h * MLA_PAD + HEAD_DIM:(h + 1) * MLA_PAD] = _rope_lanes(
            y[:, HEAD_DIM:], cos_t, sin_t).astype(q_out.dtype)

    ckv = p_ref[:, q_lora:].astype(F32)
    ckvn = (_rms(ckv, ckv.shape[-1]) * kva_ref[...]).astype(BF16)
    kn = _dot(ckvn, wkn_ref[...])
    v_out[...] = _dot(ckvn, wv_ref[...]).astype(v_out.dtype)
    kr = kr_ref[...]
    lane = lax.broadcasted_iota(jnp.int32, kr.shape, 1)
    kr = jnp.where(lane < MLA_ROPE, kr, 0.0)
    kr_ss = jnp.sum(kr * kr, axis=-1, keepdims=True)
    for h in range(N_HEADS):
        kb = kn[:, h * HEAD_DIM:(h + 1) * HEAD_DIM]
        inv = lax.rsqrt((jnp.sum(kb * kb, axis=-1, keepdims=True) + kr_ss) / MLA_QK + EPS)
        k_out[:, h * MLA_PAD:h * MLA_PAD + HEAD_DIM] = (kb * inv * gk[:, :HEAD_DIM]).astype(k_out.dtype)
        k_out[:, h * MLA_PAD + HEAD_DIM:(h + 1) * MLA_PAD] = _rope_lanes(
            kr * inv * gk[:, HEAD_DIM:], cos_t, sin_t).astype(k_out.dtype)


def mla_prep(proj, kr, qa, kva, wq, wkn, wv, gq, gk, cos_t, sin_t, seq, q_lora, kv_lora, tm=512):
    m = proj.shape[0]
    lat = q_lora + kv_lora
    lat_blk = (3 * N_HEADS * HEAD_DIM) // lat
    assert lat_blk * lat == 3 * N_HEADS * HEAD_DIM
    ns = seq // tm
    full = lambda i: (0, 0)
    kern = functools.partial(_mla_prep_kernel, q_lora=q_lora)
    return pl.pallas_call(
        kern,
        grid=(m // tm,),
        in_specs=[pl.BlockSpec((tm, lat), lambda i: (i, lat_blk)),
                  pl.BlockSpec((tm, HEAD_DIM), lambda i: (i, 0)),
                  pl.BlockSpec((1, q_lora), full),
                  pl.BlockSpec((1, kv_lora), full),
                  pl.BlockSpec(wq.shape, full),
                  pl.BlockSpec(wkn.shape, full),
                  pl.BlockSpec(wv.shape, full),
                  pl.BlockSpec((1, MLA_PAD), full),
                  pl.BlockSpec((1, MLA_PAD), full),
                  pl.BlockSpec((tm, HEAD_DIM), lambda i: (i % ns, 0)),
                  pl.BlockSpec((tm, HEAD_DIM), lambda i: (i % ns, 0))],
        out_specs=[pl.BlockSpec((tm, N_HEADS * MLA_PAD), lambda i: (i, 0)),
                   pl.BlockSpec((tm, N_HEADS * MLA_PAD), lambda i: (i, 0)),
                   pl.BlockSpec((tm, N_HEADS * HEAD_DIM), lambda i: (i, 0))],
        out_shape=[jax.ShapeDtypeStruct((m, N_HEADS * MLA_PAD), BF16),
                   jax.ShapeDtypeStruct((m, N_HEADS * MLA_PAD), BF16),
                   jax.ShapeDtypeStruct((m, N_HEADS * HEAD_DIM), BF16)],
        compiler_params=_params("parallel"),
        name="mla_prep",
    )(proj, kr, qa, kva, wq, wkn, wv, gq, gk, cos_t, sin_t)


def _mla_attn_kernel(q_ref, k_ref, v_ref, o_ref, *, blk, scale):
    qi = pl.program_id(2)
    q = q_ref[...]
    row = lax.broadcasted_iota(jnp.int32, (blk, blk), 0)
    col = lax.broadcasted_iota(jnp.int32, (blk, blk), 1)
    causal = col <= row

    def step(j, m, l, acc, masked):
        ks = k_ref[pl.ds(pl.multiple_of(j * blk, blk), blk), :]
        vs = v_ref[pl.ds(pl.multiple_of(j * blk, blk), blk), :]
        z = _dot_nt(q, ks) * scale
        if masked:
            z = jnp.where(causal, z, MASK_NEG)
        m_new = jnp.maximum(m, jnp.max(z, axis=1, keepdims=True))
        p = jnp.exp(z - m_new)
        alpha = jnp.exp(m - m_new)
        l = alpha * l + jnp.sum(p, axis=1, keepdims=True)
        acc = alpha * acc + _dot(p.astype(BF16), vs)
        return m_new, l, acc

    def body(j, c):
        return step(j, c[0], c[1], c[2], False)

    init = (jnp.full((blk, 1), MASK_NEG, F32), jnp.zeros((blk, 1), F32), jnp.zeros((blk, HEAD_DIM), F32))
    m, l, acc = lax.fori_loop(0, qi, body, init)
    m, l, acc = step(qi, m, l, acc, True)
    o_ref[...] = (acc / l).astype(o_ref.dtype)


def mla_attention(q, k, v, batch, seq, blk=256):
    nq = seq // blk
    kern = functools.partial(_mla_attn_kernel, blk=blk, scale=MLA_QK ** -0.5)
    return pl.pallas_call(
        kern,
        grid=(batch, N_HEADS, nq),
        in_specs=[pl.BlockSpec((blk, MLA_PAD), lambda b, h, i: (b * nq + i, h)),
                  pl.BlockSpec((seq, MLA_PAD), lambda b, h, i: (b, h)),
                  pl.BlockSpec((seq, HEAD_DIM), lambda b, h, i: (b, h))],
        out_specs=pl.BlockSpec((blk, HEAD_DIM), lambda b, h, i: (b * nq + i, h)),
        out_shape=jax.ShapeDtypeStruct((batch * seq, N_HEADS * HEAD_DIM), BF16),
        compiler_params=_params("parallel", "parallel", "arbitrary"),
        name="mla_attention",
    )(q, k, v)


def _seg_cumsum(x, seg, row):
    pos = row & (seg - 1)
    s = 1
    while s < seg:
        x = x + jnp.where(pos >= s, pltpu.roll(x, s, 0), 0.0)
        s *= 2
    return x


def _seg_rev_cumsum_excl(x, seg, row):
    pos = row & (seg - 1)
    n = x.shape[0]
    y = x
    s = 1
    while s < seg:
        y = y + jnp.where(pos < seg - s, pltpu.roll(y, n - s, 0), 0.0)
        s *= 2
    return y - x


def _hgrn2_kernel(lb_ref, q_ref, z_ref, i_ref, g_ref, gn_ref, o_ref, st_ref, *, rows, layer):
    @pl.when(pl.program_id(2) == 0)
    def _():
        st_ref[...] = jnp.zeros_like(st_ref)

    lbp = lb_ref[...]
    e = jnp.exp(lbp - jnp.max(lbp, axis=0, keepdims=True))
    p = e / jnp.sum(e, axis=0, keepdims=True)
    lb = jnp.sum(p[:layer + 1], axis=0, keepdims=True) - p[0:1]

    z = z_ref[...]
    q = q_ref[...]
    v = i_ref[...]
    log_sig = jnp.minimum(z, 0.0) - jnp.log1p(jnp.exp(-jnp.abs(z)))
    a = jnp.log(jnp.maximum(lb, LB_FLOOR))
    c = jnp.log1p(-lb) + log_sig
    log_f = jnp.maximum(a, c) + jnp.log1p(jnp.exp(-jnp.abs(a - c)))
    k = 1.0 - jnp.exp(log_f)

    row = lax.broadcasted_iota(jnp.int32, (rows, HEAD_DIM), 0)
    b = _seg_cumsum(log_f, HG_CHUNK, row)
    b_rev = _seg_rev_cumsum_excl(log_f, HG_CHUNK, row)
    q_in = (q * jnp.exp(b)).astype(BF16)
    k_out = (k * jnp.exp(b_rev)).astype(BF16)
    v16 = v.astype(BF16)
    diag = jnp.sum(q * k, axis=1, keepdims=True)

    levels = []
    half = 1
    while half < HG_CHUNK:
        second = (row & (2 * half - 1)) >= half
        eq = _seg_cumsum(log_f, half, row) if half > 1 else log_f
        ql = jnp.where(second, q * jnp.exp(eq), 0.0).astype(BF16)
        if half > 1:
            kl = k * jnp.exp(_seg_rev_cumsum_excl(log_f, half, row))
        else:
            kl = k
        kl = jnp.where(second, 0.0, kl).astype(BF16)
        levels.append((2 * half, ql, kl))
        half *= 2

    ct = lax.broadcasted_iota(jnp.int32, (HG_CHUNK, HG_CHUNK), 0)
    cs = lax.broadcasted_iota(jnp.int32, (HG_CHUNK, HG_CHUNK), 1)
    st = st_ref[...]
    outs = []
    for ci in range(rows // HG_CHUNK):
        sl = slice(ci * HG_CHUNK, (ci + 1) * HG_CHUNK)
        attn = jnp.zeros((HG_CHUNK, HG_CHUNK), F32)
        for w, ql, kl in levels:
            part = _dot_nt(ql[sl], kl[sl])
            if w < HG_CHUNK:
                part = jnp.where((ct & -w) == (cs & -w), part, 0.0)
            attn = attn + part
        o = _dot(attn.astype(BF16), v16[sl]) + diag[sl] * v[sl] + _dot_nt(q_in[sl], st.astype(BF16))
        outs.append(o)
        b_last = b[(ci + 1) * HG_CHUNK - 1:(ci + 1) * HG_CHUNK, :]
        st = st * jnp.exp(b_last) + _dot_tn(v16[sl], k_out[sl])
    st_ref[...] = st

    o = jnp.concatenate(outs, axis=0)
    g = g_ref[...]
    o_ref[...] = (_rms(o, HEAD_DIM) * gn_ref[...] * (g * jax.nn.sigmoid(g))).astype(o_ref.dtype)


def hgrn2(proj, lower_bounds, out_norm, layer, batch, seq, rows=256):
    depth = lower_bounds.shape[0]
    nr = seq // rows
    kern = functools.partial(_hgrn2_kernel, rows=rows, layer=layer)

    def col(c):
        return pl.BlockSpec((rows, HEAD_DIM), lambda b, h, r: (b * nr + r, c * N_HEADS + h))

    return pl.pallas_call(
        kern,
        grid=(batch, N_HEADS, nr),
        in_specs=[pl.BlockSpec((depth, HEAD_DIM), lambda b, h, r: (0, h)),
                  col(0), col(1), col(2), col(3),
                  pl.BlockSpec((1, HEAD_DIM), lambda b, h, r: (0, 0))],
        out_specs=pl.BlockSpec((rows, HEAD_DIM), lambda b, h, r: (b * nr + r, h)),
        out_shape=jax.ShapeDtypeStruct((batch * seq, N_HEADS * HEAD_DIM), BF16),
        scratch_shapes=[pltpu.VMEM((HEAD_DIM, HEAD_DIM), F32)],
        compiler_params=_params("parallel", "parallel", "arbitrary"),
        name="hgrn2",
    )(lower_bounds, proj, proj, proj, proj, out_norm.reshape(1, HEAD_DIM))


def _merge_kernel(a0, a1, a2, w0, w1, w2, g0, g1, g2, o_ref):
    acc = jax.nn.sigmoid(g0[...].astype(F32)) * _dot(a0[...], w0[...])
    acc = acc + jax.nn.sigmoid(g1[...].astype(F32)) * _dot(a1[...], w1[...])
    acc = acc + jax.nn.sigmoid(g2[...].astype(F32)) * _dot(a2[...], w2[...])
    o_ref[...] = acc.astype(o_ref.dtype)


def gated_merge(o_sb, o_mla, o_hg, w_sb, w_mla, w_hg, gate_logits, tm=1024, tn=512):
    m, k = o_sb.shape
    n = w_sb.shape[1]
    nb = n // tn
    a_spec = pl.BlockSpec((tm, k), lambda i, j: (i, 0))
    w_spec = pl.BlockSpec((k, tn), lambda i, j: (0, j))

    def g_spec(branch):
        return pl.BlockSpec((tm, tn), lambda i, j: (i, branch * nb + j))

    return pl.pallas_call(
        _merge_kernel,
        grid=(m // tm, nb),
        in_specs=[a_spec, a_spec, a_spec, w_spec, w_spec, w_spec, g_spec(0), g_spec(1), g_spec(2)],
        out_specs=pl.BlockSpec((tm, tn), lambda i, j: (i, j)),
        out_shape=jax.ShapeDtypeStruct((m, n), BF16),
        compiler_params=_params("parallel", "arbitrary"),
        name="gated_merge",
    )(o_sb, o_mla, o_hg, w_sb, w_mla, w_hg, gate_logits, gate_logits, gate_logits)


HALO = 16


def _up_conv_glu_kernel(a_ref, ah_ref, wg_ref, wv_ref, cg_ref, cv_ref, o_ref, *, tiles_per_seq):
    a = a_ref[...]
    ah = ah_ref[...]
    seq_start = (pl.program_id(0) % tiles_per_seq) == 0
    row = lax.broadcasted_iota(jnp.int32, o_ref.shape, 0)

    def conv(w_ref, c_ref):
        w = w_ref[...]
        u = _dot(a, w)
        halo = jnp.where(seq_start, 0.0, _dot(ah, w))
        prev1 = halo[HALO - 1:HALO]
        prev2 = halo[HALO - 2:HALO - 1]
        u1 = jnp.where(row == 0, prev1, pltpu.roll(u, 1, 0))
        u2 = jnp.where(row == 0, prev2, jnp.where(row == 1, prev1, pltpu.roll(u, 2, 0)))
        c = c_ref[...]
        return u2 * c[0:1] + u1 * c[1:2] + u * c[2:3]

    gate = conv(wg_ref, cg_ref)
    val = conv(wv_ref, cv_ref)
    o_ref[...] = (gate * jax.nn.sigmoid(gate) * val).astype(o_ref.dtype)


def up_conv_glu(h, w_up, conv_w, seq, tm=1024, tn=512):
    m, k = h.shape
    d_ff = w_up.shape[1] // 2
    nb = d_ff // tn
    kern = functools.partial(_up_conv_glu_kernel, tiles_per_seq=seq // tm)
    halo_blocks = tm // HALO
    return pl.pallas_call(
        kern,
        grid=(m // tm, nb),
        in_specs=[pl.BlockSpec((tm, k), lambda i, j: (i, 0)),
                  pl.BlockSpec((HALO, k), lambda i, j: (jnp.maximum(i * halo_blocks - 1, 0), 0)),
                  pl.BlockSpec((k, tn), lambda i, j: (0, j)),
                  pl.BlockSpec((k, tn), lambda i, j: (0, nb + j)),
                  pl.BlockSpec((conv_w.shape[0], tn), lambda i, j: (0, j)),
                  pl.BlockSpec((conv_w.shape[0], tn), lambda i, j: (0, nb + j))],
        out_specs=pl.BlockSpec((tm, tn), lambda i, j: (i, j)),
        out_shape=jax.ShapeDtypeStruct((m, d_ff), BF16),
        compiler_params=_params("parallel", "arbitrary"),
        name="up_conv_glu",
    )(h, h, w_up, w_up, conv_w, conv_w)


def _rope_lane_tables(seq):
    pos = jnp.arange(seq, dtype=F32)
    inv = ROPE_THETA ** (-jnp.arange(0, MLA_ROPE, 2, dtype=F32) / MLA_ROPE)
    ang = pos[:, None] * inv[None, :]
    cos, sin = jnp.cos(ang), jnp.sin(ang)
    zeros = jnp.zeros((seq, HEAD_DIM - MLA_ROPE), F32)
    return (jnp.concatenate([cos, cos, zeros], axis=1),
            jnp.concatenate([-sin, sin, zeros], axis=1))


def _pad_head_gain(g):
    return jnp.pad(g, (0, MLA_PAD - MLA_QK)).reshape(1, MLA_PAD)


def kernel(x, lower_bounds, attn_norm, w_in, sb_q_norm, sb_k_norm, mla_q_a_norm, mla_kv_a_norm, mla_w_q_b, mla_w_kv_b, mla_q_norm, mla_k_norm, hg_out_norm, w_branch_sb, w_branch_mla, w_branch_hg, w_out, ffn_norm, w_up, ffn_conv, w_down):
    batch, seq, d_model = x.shape
    depth = w_in.shape[0]
    width = N_HEADS * HEAD_DIM
    q_lora = mla_w_q_b.shape[1]
    kv_lora = mla_w_kv_b.shape[1]
    c_attn = 3 * width + q_lora + kv_lora
    c_hg = c_attn + MLA_ROPE
    c_gate = c_hg + 4 * width

    cos_t, sin_t = _rope_lane_tables(seq)
    xf = x.reshape(batch * seq, d_model)
    for l in range(depth):
        wl = w_in[l]
        w_attn = wl[:, :c_attn].astype(BF16)
        w_kr = wl[:, c_attn:c_attn + HEAD_DIM].astype(BF16)
        w_hgp = wl[:, c_hg:c_gate].astype(BF16)
        w_gate = wl[:, c_gate:].astype(BF16)
        wq = jnp.pad(mla_w_q_b[l], ((0, 0), (0, 0), (0, MLA_PAD - MLA_QK))).reshape(q_lora, N_HEADS * MLA_PAD)
        wkn = mla_w_kv_b[l][:, :, :HEAD_DIM].reshape(kv_lora, width)
        wv = mla_w_kv_b[l][:, :, HEAD_DIM:].reshape(kv_lora, width)

        h = rmsnorm_rows(xf, attn_norm[l])
        p_attn = matmul(h, w_attn, BF16, 1024, 768)
        p_kr = matmul(h, w_kr, F32, 1024, HEAD_DIM)
        p_hg = matmul(h, w_hgp, F32, 1024, 1024)
        p_gate = matmul(h, w_gate, BF16, 1024, 1024)

        sb_qk = headnorm_qk(p_attn, jnp.stack([sb_q_norm[l], sb_k_norm[l]]).reshape(2, 1, HEAD_DIM), width)
        o_sb = sb_attention(sb_qk, p_attn, batch, seq)

        mq, mk, mv = mla_prep(p_attn, p_kr, mla_q_a_norm[l].reshape(1, q_lora),
                              mla_kv_a_norm[l].reshape(1, kv_lora), wq.astype(BF16), wkn.astype(BF16),
                              wv.astype(BF16), _pad_head_gain(mla_q_norm[l]), _pad_head_gain(mla_k_norm[l]),
                              cos_t, sin_t, seq, q_lora, kv_lora)
        o_mla = mla_attention(mq, mk, mv, batch, seq)

        o_hg = hgrn2(p_hg, lower_bounds, hg_out_norm[l], l, batch, seq)

        merged = gated_merge(o_sb, o_mla, o_hg, w_branch_sb[l].astype(BF16), w_branch_mla[l].astype(BF16),
                             w_branch_hg[l].astype(BF16), p_gate)
        xf = matmul_residual(merged, w_out[l].astype(BF16), xf, 1024, 512, d_model)

        h2 = rmsnorm_rows(xf, ffn_norm[l])
        act = up_conv_glu(h2, w_up[l].astype(BF16), ffn_conv[l], seq)
        xf = matmul_residual(act, w_down[l].astype(BF16), xf, 1024, 256, act.shape[1])
    return xf.reshape(batch, seq, d_model)
```

```python
import functools

import jax
import jax.numpy as jnp
from jax import lax
from jax.experimental import pallas as pl
from jax.experimental.pallas import tpu as pltpu

F32 = jnp.float32
BF16 = jnp.bfloat16

HEAD_DIM = 128
N_HEADS = 8
MLA_ROPE = 64
MLA_QK = HEAD_DIM + MLA_ROPE
MLA_PAD = 256
HG_CHUNK = 64
ROPE_THETA = 10000.0
EPS = 1e-6
MASK_NEG = -1e30
LB_FLOOR = 1e-30

VMEM_LIMIT_BYTES = 56 * 1024 * 1024


def _params(*sem):
    return pltpu.CompilerParams(dimension_semantics=sem, vmem_limit_bytes=VMEM_LIMIT_BYTES)


def _dot(a, b):
    return jnp.dot(a, b, preferred_element_type=F32)


def _dot_nt(a, b):
    return lax.dot_general(a, b, (((1,), (1,)), ((), ())), preferred_element_type=F32)


def _dot_tn(a, b):
    return lax.dot_general(a, b, (((0,), (0,)), ((), ())), preferred_element_type=F32)


def _rms(x, width):
    return x * lax.rsqrt(jnp.sum(x * x, axis=-1, keepdims=True) / width + EPS)


def _rmsnorm_kernel(x_ref, g_ref, o_ref):
    x = x_ref[...]
    o_ref[...] = (_rms(x, x.shape[-1]) * g_ref[...]).astype(o_ref.dtype)


def rmsnorm_rows(x, g, tm=256):
    m, d = x.shape
    return pl.pallas_call(
        _rmsnorm_kernel,
        grid=(m // tm,),
        in_specs=[pl.BlockSpec((tm, d), lambda i: (i, 0)),
                  pl.BlockSpec((1, d), lambda i: (0, 0))],
        out_specs=pl.BlockSpec((tm, d), lambda i: (i, 0)),
        out_shape=jax.ShapeDtypeStruct((m, d), BF16),
        compiler_params=_params("parallel"),
        name="rmsnorm_rows",
    )(x, g.reshape(1, d))


def _mm_kernel(a_ref, b_ref, o_ref):
    o_ref[...] = _dot(a_ref[...], b_ref[...]).astype(o_ref.dtype)


def matmul(a, b, out_dtype, tm, tn):
    m, k = a.shape
    n = b.shape[1]
    return pl.pallas_call(
        _mm_kernel,
        grid=(m // tm, n // tn),
        in_specs=[pl.BlockSpec((tm, k), lambda i, j: (i, 0)),
                  pl.BlockSpec((k, tn), lambda i, j: (0, j))],
        out_specs=pl.BlockSpec((tm, tn), lambda i, j: (i, j)),
        out_shape=jax.ShapeDtypeStruct((m, n), out_dtype),
        compiler_params=_params("parallel", "arbitrary"),
        name="matmul",
    )(a, b)


def _mm_res_kernel(a_ref, b_ref, r_ref, o_ref):
    d = _dot(a_ref[...], b_ref[...])

    @pl.when(pl.program_id(2) == 0)
    def _():
        o_ref[...] = r_ref[...] + d

    @pl.when(pl.program_id(2) != 0)
    def _():
        o_ref[...] += d


def matmul_residual(a, b, res, tm, tn, tk):
    m, k = a.shape
    n = b.shape[1]
    return pl.pallas_call(
        _mm_res_kernel,
        grid=(m // tm, n // tn, k // tk),
        in_specs=[pl.BlockSpec((tm, tk), lambda i, j, kk: (i, kk)),
                  pl.BlockSpec((tk, tn), lambda i, j, kk: (kk, j)),
                  pl.BlockSpec((tm, tn), lambda i, j, kk: (i, j))],
        out_specs=pl.BlockSpec((tm, tn), lambda i, j, kk: (i, j)),
        out_shape=jax.ShapeDtypeStruct((m, n), F32),
        compiler_params=_params("parallel", "arbitrary", "arbitrary"),
        name="matmul_residual",
    )(a, b, res)


def _headnorm_kernel(x_ref, g_ref, o_ref):
    g = g_ref[0]
    for h in range(x_ref.shape[1] // HEAD_DIM):
        sl = slice(h * HEAD_DIM, (h + 1) * HEAD_DIM)
        x = x_ref[:, sl].astype(F32)
        o_ref[:, sl] = (_rms(x, HEAD_DIM) * g).astype(o_ref.dtype)


def headnorm_qk(proj, gains, width, tm=512):
    m = proj.shape[0]
    return pl.pallas_call(
        _headnorm_kernel,
        grid=(m // tm, 2),
        in_specs=[pl.BlockSpec((tm, width), lambda i, j: (i, j)),
                  pl.BlockSpec((1, 1, HEAD_DIM), lambda i, j: (j, 0, 0))],
        out_specs=pl.BlockSpec((tm, width), lambda i, j: (i, j)),
        out_shape=jax.ShapeDtypeStruct((m, 2 * width), BF16),
        compiler_params=_params("parallel", "arbitrary"),
        name="headnorm_qk",
    )(proj, gains)


def _kv_block(k_ref, v_ref, j, blk):
    start = pl.multiple_of(j * blk, blk)
    return k_ref[pl.ds(start, blk), :], v_ref[pl.ds(start, blk), :]


def _sb_block(q, ks, vs, carry, acc, from_here, scale, strict):
    z = _dot_nt(q, ks) * scale
    log_1m_beta = -(jnp.maximum(z, 0.0) + jnp.log(1.0 + jnp.exp(-jnp.abs(z))))
    if strict is not None:
        log_1m_beta = jnp.where(strict, log_1m_beta, 0.0)
    hi = log_1m_beta.astype(BF16)
    lo = (log_1m_beta - hi.astype(F32)).astype(BF16)
    incl = _dot(jnp.concatenate([hi, lo], axis=1), from_here)
    w = jnp.exp(z + incl + carry)
    if strict is not None:
        w = jnp.where(strict, w, 0.0)
    acc = acc + _dot(w.astype(BF16), vs)
    carry = carry + jnp.sum(log_1m_beta, axis=1, keepdims=True)
    return carry, acc


def _sb_attn_kernel(q_ref, k_ref, v_ref, o_ref, *, blk, scale):
    qi = pl.program_id(2)
    qa = q_ref[:blk, :]
    qb = q_ref[blk:, :]
    row = lax.broadcasted_iota(jnp.int32, (blk, blk), 0)
    col = lax.broadcasted_iota(jnp.int32, (blk, blk), 1)
    strict = col < row
    tri = (row >= col).astype(BF16)
    from_here = jnp.concatenate([tri, tri], axis=0)
    block = functools.partial(_sb_block, from_here=from_here, scale=scale)
    zero_c = jnp.zeros((blk, 1), F32)
    zero_a = jnp.zeros((blk, HEAD_DIM), F32)

    ks, vs = _kv_block(k_ref, v_ref, 2 * qi + 1, blk)
    cb, ab = block(qb, ks, vs, zero_c, zero_a, strict=strict)
    ks, vs = _kv_block(k_ref, v_ref, 2 * qi, blk)
    ca, aa = block(qa, ks, vs, zero_c, zero_a, strict=strict)
    cb, ab = block(qb, ks, vs, cb, ab, strict=None)

    def body(t, c):
        ca, aa, cb, ab = c
        for u in range(2):
            ks, vs = _kv_block(k_ref, v_ref, 2 * (qi - t) - 1 - u, blk)
            ca, aa = block(qa, ks, vs, ca, aa, strict=None)
            cb, ab = block(qb, ks, vs, cb, ab, strict=None)
        return ca, aa, cb, ab

    ca, aa, cb, ab = lax.fori_loop(0, qi, body, (ca, aa, cb, ab))
    o_ref[:blk, :] = aa.astype(o_ref.dtype)
    o_ref[blk:, :] = ab.astype(o_ref.dtype)


def sb_attention(qk, proj, batch, seq, blk=256):
    width = N_HEADS * HEAD_DIM
    nq = seq // (2 * blk)
    kern = functools.partial(_sb_attn_kernel, blk=blk, scale=HEAD_DIM ** -0.5)
    return pl.pallas_call(
        kern,
        grid=(batch, N_HEADS, nq),
        in_specs=[pl.BlockSpec((2 * blk, HEAD_DIM), lambda b, h, i: (b * nq + i, h)),
                  pl.BlockSpec((seq, HEAD_DIM), lambda b, h, i: (b, N_HEADS + h)),
                  pl.BlockSpec((seq, HEAD_DIM), lambda b, h, i: (b, 2 * N_HEADS + h))],
        out_specs=pl.BlockSpec((2 * blk, HEAD_DIM), lambda b, h, i: (b * nq + i, h)),
        out_shape=jax.ShapeDtypeStruct((batch * seq, width), BF16),
        compiler_params=_params("parallel", "parallel", "arbitrary"),
        name="sb_attention",
    )(qk, qk, proj)


def _rope_lanes(x, cos_t, sin_t):
    lane = lax.broadcasted_iota(jnp.int32, x.shape, 1)
    half = MLA_ROPE // 2
    partner = jnp.where(lane < half, pltpu.roll(x, HEAD_DIM - half, 1), pltpu.roll(x, half, 1))
    return x * cos_t + partner * sin_t


def _mla_prep_kernel(p_ref, kr_ref, qa_ref, kva_ref, wq_ref, wkn_ref, wv_ref, gq_ref, gk_ref,
                     cos_ref, sin_ref, q_out, k_out, v_out, *, q_lora):
    cos_t = cos_ref[...]
    sin_t = sin_ref[...]
    gq = gq_ref[...]
    gk = gk_ref[...]
    cq = p_ref[:, :q_lora].astype(F32)
    cqn = (_rms(cq, q_lora) * qa_ref[...]).astype(BF16)
    qraw = _dot(cqn, wq_ref[...])
    for h in range(N_HEADS):
        blk = qraw[:, h * MLA_PAD:(h + 1) * MLA_PAD]
        y = _rms(blk, MLA_QK) * gq
        q_out[:, h * MLA_PAD:h * MLA_PAD + HEAD_DIM] = y[:, :HEAD_DIM].astype(q_out.dtype)
        q_out[:, h * MLA_PAD + HEAD_DIM:(h + 1) * MLA_PAD] = _rope_lanes(
            y[:, HEAD_DIM:], cos_t, sin_t).astype(q_out.dtype)

    ckv = p_ref[:, q_lora:].astype(F32)
    ckvn = (_rms(ckv, ckv.shape[-1]) * kva_ref[...]).astype(BF16)
    kn = _dot(ckvn, wkn_ref[...])
    v_out[...] = _dot(ckvn, wv_ref[...]).astype(v_out.dtype)
    kr = kr_ref[...]
    lane = lax.broadcasted_iota(jnp.int32, kr.shape, 1)
    kr = jnp.where(lane < MLA_ROPE, kr, 0.0)
    kr_ss = jnp.sum(kr * kr, axis=-1, keepdims=True)
    for h in range(N_HEADS):
        kb = kn[:, h * HEAD_DIM:(h + 1) * HEAD_DIM]
        inv = lax.rsqrt((jnp.sum(kb * kb, axis=-1, keepdims=True) + kr_ss) / MLA_QK + EPS)
        k_out[:, h * MLA_PAD:h * MLA_PAD + HEAD_DIM] = (kb * inv * gk[:, :HEAD_DIM]).astype(k_out.dtype)
        k_out[:, h * MLA_PAD + HEAD_DIM:(h + 1) * MLA_PAD] = _rope_lanes(
            kr * inv * gk[:, HEAD_DIM:], cos_t, sin_t).astype(k_out.dtype)


def mla_prep(proj, kr, qa, kva, wq, wkn, wv, gq, gk, cos_t, sin_t, seq, q_lora, kv_lora, tm=512):
    m = proj.shape[0]
    lat = q_lora + kv_lora
    lat_blk = (3 * N_HEADS * HEAD_DIM) // lat
    assert lat_blk * lat == 3 * N_HEADS * HEAD_DIM
    ns = seq // tm
    full = lambda i: (0, 0)
    kern = functools.partial(_mla_prep_kernel, q_lora=q_lora)
    return pl.pallas_call(
        kern,
        grid=(m // tm,),
        in_specs=[pl.BlockSpec((tm, lat), lambda i: (i, lat_blk)),
                  pl.BlockSpec((tm, HEAD_DIM), lambda i: (i, 0)),
                  pl.BlockSpec((1, q_lora), full),
                  pl.BlockSpec((1, kv_lora), full),
                  pl.BlockSpec(wq.shape, full),
                  pl.BlockSpec(wkn.shape, full),
                  pl.BlockSpec(wv.shape, full),
                  pl.BlockSpec((1, MLA_PAD), full),
                  pl.BlockSpec((1, MLA_PAD), full),
                  pl.BlockSpec((tm, HEAD_DIM), lambda i: (i % ns, 0)),
                  pl.BlockSpec((tm, HEAD_DIM), lambda i: (i % ns, 0))],
        out_specs=[pl.BlockSpec((tm, N_HEADS * MLA_PAD), lambda i: (i, 0)),
                   pl.BlockSpec((tm, N_HEADS * MLA_PAD), lambda i: (i, 0)),
                   pl.BlockSpec((tm, N_HEADS * HEAD_DIM), lambda i: (i, 0))],
        out_shape=[jax.ShapeDtypeStruct((m, N_HEADS * MLA_PAD), BF16),
                   jax.ShapeDtypeStruct((m, N_HEADS * MLA_PAD), BF16),
                   jax.ShapeDtypeStruct((m, N_HEADS * HEAD_DIM), BF16)],
        compiler_params=_params("parallel"),
        name="mla_prep",
    )(proj, kr, qa, kva, wq, wkn, wv, gq, gk, cos_t, sin_t)


def _softmax_block(q, ks, vs, state, scale, causal):
    m, l, acc = state
    z = _dot_nt(q, ks)
    if causal is not None:
        z = jnp.where(causal, z, MASK_NEG)
    m_new = jnp.maximum(m, jnp.max(z, axis=1, keepdims=True))
    p = jnp.exp((z - m_new) * scale)
    alpha = jnp.exp((m - m_new) * scale)
    l = alpha * l + jnp.sum(p, axis=1, keepdims=True)
    acc = alpha * acc + _dot(p.astype(BF16), vs)
    return m_new, l, acc


def _mla_attn_kernel(q_ref, k_ref, v_ref, o_ref, *, blk, scale):
    qi = pl.program_id(2)
    qa = q_ref[:blk, :]
    qb = q_ref[blk:, :]
    row = lax.broadcasted_iota(jnp.int32, (blk, blk), 0)
    col = lax.broadcasted_iota(jnp.int32, (blk, blk), 1)
    causal = col <= row
    block = functools.partial(_softmax_block, scale=scale)
    init = (jnp.full((blk, 1), MASK_NEG, F32), jnp.zeros((blk, 1), F32), jnp.zeros((blk, HEAD_DIM), F32))

    def body(t, c):
        sa, sb = c
        for u in range(2):
            ks, vs = _kv_block(k_ref, v_ref, 2 * t + u, blk)
            sa = block(qa, ks, vs, sa, causal=None)
            sb = block(qb, ks, vs, sb, causal=None)
        return sa, sb

    sa, sb = lax.fori_loop(0, qi, body, (init, init))
    ks, vs = _kv_block(k_ref, v_ref, 2 * qi, blk)
    sa = block(qa, ks, vs, sa, causal=causal)
    sb = block(qb, ks, vs, sb, causal=None)
    ks, vs = _kv_block(k_ref, v_ref, 2 * qi + 1, blk)
    sb = block(qb, ks, vs, sb, causal=causal)
    o_ref[:blk, :] = (sa[2] / sa[1]).astype(o_ref.dtype)
    o_ref[blk:, :] = (sb[2] / sb[1]).astype(o_ref.dtype)


def mla_attention(q, k, v, batch, seq, blk=256):
    nq = seq // (2 * blk)
    kern = functools.partial(_mla_attn_kernel, blk=blk, scale=MLA_QK ** -0.5)
    return pl.pallas_call(
        kern,
        grid=(batch, N_HEADS, nq),
        in_specs=[pl.BlockSpec((2 * blk, MLA_PAD), lambda b, h, i: (b * nq + i, h)),
                  pl.BlockSpec((seq, MLA_PAD), lambda b, h, i: (b, h)),
                  pl.BlockSpec((seq, HEAD_DIM), lambda b, h, i: (b, h))],
        out_specs=pl.BlockSpec((2 * blk, HEAD_DIM), lambda b, h, i: (b * nq + i, h)),
        out_shape=jax.ShapeDtypeStruct((batch * seq, N_HEADS * HEAD_DIM), BF16),
        compiler_params=_params("parallel", "parallel", "arbitrary"),
        name="mla_attention",
    )(q, k, v)


def _seg_cumsum(x, seg, row):
    pos = row & (seg - 1)
    s = 1
    while s < seg:
        x = x + jnp.where(pos >= s, pltpu.roll(x, s, 0), 0.0)
        s *= 2
    return x


def _seg_rev_cumsum_excl(x, seg, row):
    pos = row & (seg - 1)
    n = x.shape[0]
    y = x
    s = 1
    while s < seg:
        y = y + jnp.where(pos < seg - s, pltpu.roll(y, n - s, 0), 0.0)
        s *= 2
    return y - x


def _hgrn2_kernel(lb_ref, q_ref, z_ref, i_ref, g_ref, gn_ref, o_ref, st_ref, *, rows, layer):
    @pl.when(pl.program_id(2) == 0)
    def _():
        st_ref[...] = jnp.zeros_like(st_ref)

    lbp = lb_ref[...]
    e = jnp.exp(lbp - jnp.max(lbp, axis=0, keepdims=True))
    p = e / jnp.sum(e, axis=0, keepdims=True)
    lb = jnp.sum(p[:layer + 1], axis=0, keepdims=True) - p[0:1]

    z = z_ref[...]
    q = q_ref[...]
    v = i_ref[...]
    log_sig = jnp.minimum(z, 0.0) - jnp.log1p(jnp.exp(-jnp.abs(z)))
    a = jnp.log(jnp.maximum(lb, LB_FLOOR))
    c = jnp.log1p(-lb) + log_sig
    log_f = jnp.maximum(a, c) + jnp.log1p(jnp.exp(-jnp.abs(a - c)))
    k = 1.0 - jnp.exp(log_f)

    row = lax.broadcasted_iota(jnp.int32, (rows, HEAD_DIM), 0)
    b = _seg_cumsum(log_f, HG_CHUNK, row)
    b_rev = _seg_rev_cumsum_excl(log_f, HG_CHUNK, row)
    q_in = (q * jnp.exp(b)).astype(BF16)
    k_out = (k * jnp.exp(b_rev)).astype(BF16)
    v16 = v.astype(BF16)
    diag = jnp.sum(q * k, axis=1, keepdims=True)

    levels = []
    half = 1
    while half < HG_CHUNK:
        second = (row & (2 * half - 1)) >= half
        eq = _seg_cumsum(log_f, half, row) if half > 1 else log_f
        ql = jnp.where(second, q * jnp.exp(eq), 0.0).astype(BF16)
        if half > 1:
            kl = k * jnp.exp(_seg_rev_cumsum_excl(log_f, half, row))
        else:
            kl = k
        kl = jnp.where(second, 0.0, kl).astype(BF16)
        levels.append((2 * half, ql, kl))
        half *= 2

    ct = lax.broadcasted_iota(jnp.int32, (HG_CHUNK, HG_CHUNK), 0)
    cs = lax.broadcasted_iota(jnp.int32, (HG_CHUNK, HG_CHUNK), 1)
    st = st_ref[...]
    outs = []
    for ci in range(rows // HG_CHUNK):
        sl = slice(ci * HG_CHUNK, (ci + 1) * HG_CHUNK)
        attn = jnp.zeros((HG_CHUNK, HG_CHUNK), F32)
        for w, ql, kl in levels:
            part = _dot_nt(ql[sl], kl[sl])
            if w < HG_CHUNK:
                part = jnp.where((ct & -w) == (cs & -w), part, 0.0)
            attn = attn + part
        o = _dot(attn.astype(BF16), v16[sl]) + diag[sl] * v[sl] + _dot_nt(q_in[sl], st.astype(BF16))
        outs.append(o)
        b_last = b[(ci + 1) * HG_CHUNK - 1:(ci + 1) * HG_CHUNK, :]
        st = st * jnp.exp(b_last) + _dot_tn(v16[sl], k_out[sl])
    st_ref[...] = st

    o = jnp.concatenate(outs, axis=0)
    g = g_ref[...]
    o_ref[...] = (_rms(o, HEAD_DIM) * gn_ref[...] * (g * jax.nn.sigmoid(g))).astype(o_ref.dtype)


def hgrn2(proj, lower_bounds, out_norm, layer, batch, seq, rows=256):
    depth = lower_bounds.shape[0]
    nr = seq // rows
    kern = functools.partial(_hgrn2_kernel, rows=rows, layer=layer)

    def col(c):
        return pl.BlockSpec((rows, HEAD_DIM), lambda b, h, r: (b * nr + r, c * N_HEADS + h))

    return pl.pallas_call(
        kern,
        grid=(batch, N_HEADS, nr),
        in_specs=[pl.BlockSpec((depth, HEAD_DIM), lambda b, h, r: (0, h)),
                  col(0), col(1), col(2), col(3),
                  pl.BlockSpec((1, HEAD_DIM), lambda b, h, r: (0, 0))],
        out_specs=pl.BlockSpec((rows, HEAD_DIM), lambda b, h, r: (b * nr + r, h)),
        out_shape=jax.ShapeDtypeStruct((batch * seq, N_HEADS * HEAD_DIM), BF16),
        scratch_shapes=[pltpu.VMEM((HEAD_DIM, HEAD_DIM), F32)],
        compiler_params=_params("parallel", "parallel", "arbitrary"),
        name="hgrn2",
    )(lower_bounds, proj, proj, proj, proj, out_norm.reshape(1, HEAD_DIM))


def _merge_kernel(a0, a1, a2, w0, w1, w2, g0, g1, g2, o_ref):
    acc = jax.nn.sigmoid(g0[...].astype(F32)) * _dot(a0[...], w0[...])
    acc = acc + jax.nn.sigmoid(g1[...].astype(F32)) * _dot(a1[...], w1[...])
    acc = acc + jax.nn.sigmoid(g2[...].astype(F32)) * _dot(a2[...], w2[...])
    o_ref[...] = acc.astype(o_ref.dtype)


def gated_merge(o_sb, o_mla, o_hg, w_sb, w_mla, w_hg, gate_logits, tm=1024, tn=512):
    m, k = o_sb.shape
    n = w_sb.shape[1]
    nb = n // tn
    a_spec = pl.BlockSpec((tm, k), lambda i, j: (i, 0))
    w_spec = pl.BlockSpec((k, tn), lambda i, j: (0, j))

    def g_spec(branch):
        return pl.BlockSpec((tm, tn), lambda i, j: (i, branch * nb + j))

    return pl.pallas_call(
        _merge_kernel,
        grid=(m // tm, nb),
        in_specs=[a_spec, a_spec, a_spec, w_spec, w_spec, w_spec, g_spec(0), g_spec(1), g_spec(2)],
        out_specs=pl.BlockSpec((tm, tn), lambda i, j: (i, j)),
        out_shape=jax.ShapeDtypeStruct((m, n), BF16),
        compiler_params=_params("parallel", "arbitrary"),
        name="gated_merge",
    )(o_sb, o_mla, o_hg, w_sb, w_mla, w_hg, gate_logits, gate_logits, gate_logits)


HALO = 16


def _up_conv_glu_kernel(a_ref, ah_ref, wg_ref, wv_ref, cg_ref, cv_ref, o_ref, *, tiles_per_seq):
    a = a_ref[...]
    ah = ah_ref[...]
    seq_start = (pl.program_id(0) % tiles_per_seq) == 0
    row = lax.broadcasted_iota(jnp.int32, o_ref.shape, 0)

    def conv(w_ref, c_ref):
        w = w_ref[...]
        u = _dot(a, w)
        halo = jnp.where(seq_start, 0.0, _dot(ah, w))
        prev1 = halo[HALO - 1:HALO]
        prev2 = halo[HALO - 2:HALO - 1]
        u1 = jnp.where(row == 0, prev1, pltpu.roll(u, 1, 0))
        u2 = jnp.where(row == 0, prev2, jnp.where(row == 1, prev1, pltpu.roll(u, 2, 0)))
        c = c_ref[...]
        return u2 * c[0:1] + u1 * c[1:2] + u * c[2:3]

    gate = conv(wg_ref, cg_ref)
    val = conv(wv_ref, cv_ref)
    o_ref[...] = (gate * jax.nn.sigmoid(gate) * val).astype(o_ref.dtype)


def up_conv_glu(h, w_up, conv_w, seq, tm=1024, tn=512):
    m, k = h.shape
    d_ff = w_up.shape[1] // 2
    nb = d_ff // tn
    kern = functools.partial(_up_conv_glu_kernel, tiles_per_seq=seq // tm)
    halo_blocks = tm // HALO
    return pl.pallas_call(
        kern,
        grid=(m // tm, nb),
        in_specs=[pl.BlockSpec((tm, k), lambda i, j: (i, 0)),
                  pl.BlockSpec((HALO, k), lambda i, j: (jnp.maximum(i * halo_blocks - 1, 0), 0)),
                  pl.BlockSpec((k, tn), lambda i, j: (0, j)),
                  pl.BlockSpec((k, tn), lambda i, j: (0, nb + j)),
                  pl.BlockSpec((conv_w.shape[0], tn), lambda i, j: (0, j)),
                  pl.BlockSpec((conv_w.shape[0], tn), lambda i, j: (0, nb + j))],
        out_specs=pl.BlockSpec((tm, tn), lambda i, j: (i, j)),
        out_shape=jax.ShapeDtypeStruct((m, d_ff), BF16),
        compiler_params=_params("parallel", "arbitrary"),
        name="up_conv_glu",
    )(h, h, w_up, w_up, conv_w, conv_w)


def _rope_lane_tables(seq):
    pos = jnp.arange(seq, dtype=F32)
    inv = ROPE_THETA ** (-jnp.arange(0, MLA_ROPE, 2, dtype=F32) / MLA_ROPE)
    ang = pos[:, None] * inv[None, :]
    cos, sin = jnp.cos(ang), jnp.sin(ang)
    zeros = jnp.zeros((seq, HEAD_DIM - MLA_ROPE), F32)
    return (jnp.concatenate([cos, cos, zeros], axis=1),
            jnp.concatenate([-sin, sin, zeros], axis=1))


def _pad_head_gain(g):
    return jnp.pad(g, (0, MLA_PAD - MLA_QK)).reshape(1, MLA_PAD)


def kernel(x, lower_bounds, attn_norm, w_in, sb_q_norm, sb_k_norm, mla_q_a_norm, mla_kv_a_norm, mla_w_q_b, mla_w_kv_b, mla_q_norm, mla_k_norm, hg_out_norm, w_branch_sb, w_branch_mla, w_branch_hg, w_out, ffn_norm, w_up, ffn_conv, w_down):
    batch, seq, d_model = x.shape
    depth = w_in.shape[0]
    width = N_HEADS * HEAD_DIM
    q_lora = mla_w_q_b.shape[1]
    kv_lora = mla_w_kv_b.shape[1]
    c_attn = 3 * width + q_lora + kv_lora
    c_hg = c_attn + MLA_ROPE
    c_gate = c_hg + 4 * width

    cos_t, sin_t = _rope_lane_tables(seq)
    xf = x.reshape(batch * seq, d_model)
    for l in range(depth):
        wl = w_in[l]
        w_attn = wl[:, :c_attn].astype(BF16)
        w_kr = wl[:, c_attn:c_attn + HEAD_DIM].astype(BF16)
        w_hgp = wl[:, c_hg:c_gate].astype(BF16)
        w_gate = wl[:, c_gate:].astype(BF16)
        wq = jnp.pad(mla_w_q_b[l], ((0, 0), (0, 0), (0, MLA_PAD - MLA_QK))).reshape(q_lora, N_HEADS * MLA_PAD)
        wkn = mla_w_kv_b[l][:, :, :HEAD_DIM].reshape(kv_lora, width)
        wv = mla_w_kv_b[l][:, :, HEAD_DIM:].reshape(kv_lora, width)

        h = rmsnorm_rows(xf, attn_norm[l])
        p_attn = matmul(h, w_attn, BF16, 1024, 768)
        p_kr = matmul(h, w_kr, F32, 1024, HEAD_DIM)
        p_hg = matmul(h, w_hgp, F32, 1024, 1024)
        p_gate = matmul(h, w_gate, BF16, 1024, 1024)

        sb_qk = headnorm_qk(p_attn, jnp.stack([sb_q_norm[l], sb_k_norm[l]]).reshape(2, 1, HEAD_DIM), width)
        o_sb = sb_attention(sb_qk, p_attn, batch, seq)

        mq, mk, mv = mla_prep(p_attn, p_kr, mla_q_a_norm[l].reshape(1, q_lora),
                              mla_kv_a_norm[l].reshape(1, kv_lora), wq.astype(BF16), wkn.astype(BF16),
                              wv.astype(BF16), _pad_head_gain(mla_q_norm[l]), _pad_head_gain(mla_k_norm[l]),
                              cos_t, sin_t, seq, q_lora, kv_lora)
        o_mla = mla_attention(mq, mk, mv, batch, seq)

        o_hg = hgrn2(p_hg, lower_bounds, hg_out_norm[l], l, batch, seq)

        merged = gated_merge(o_sb, o_mla, o_hg, w_branch_sb[l].astype(BF16), w_branch_mla[l].astype(BF16),
                             w_branch_hg[l].astype(BF16), p_gate)
        xf = matmul_residual(merged, w_out[l].astype(BF16), xf, 1024, 512, d_model)

        h2 = rmsnorm_rows(xf, ffn_norm[l])
        act = up_conv_glu(h2, w_up[l].astype(BF16), ffn_conv[l], seq)
        xf = matmul_residual(act, w_down[l].astype(BF16), xf, 1024, 256, act.shape[1])
    return xf.reshape(batch, seq, d_model)
```

```python
import functools

import jax
import jax.numpy as jnp
from jax import lax
from jax.experimental import pallas as pl
from jax.experimental.pallas import tpu as pltpu

F32 = jnp.float32
BF16 = jnp.bfloat16

HEAD_DIM = 128
N_HEADS = 8
MLA_ROPE = 64
MLA_QK = HEAD_DIM + MLA_ROPE
MLA_PAD = 256
HG_CHUNK = 64
ROPE_THETA = 10000.0
EPS = 1e-6
MASK_NEG = -1e30
LB_FLOOR = 1e-30

VMEM_LIMIT_BYTES = 56 * 1024 * 1024


def _params(*sem):
    return pltpu.CompilerParams(dimension_semantics=sem, vmem_limit_bytes=VMEM_LIMIT_BYTES)


def _dot(a, b):
    return jnp.dot(a, b, preferred_element_type=F32)


def _dot_nt(a, b):
    return lax.dot_general(a, b, (((1,), (1,)), ((), ())), preferred_element_type=F32)


def _dot_tn(a, b):
    return lax.dot_general(a, b, (((0,), (0,)), ((), ())), preferred_element_type=F32)


def _rms(x, width):
    return x * lax.rsqrt(jnp.sum(x * x, axis=-1, keepdims=True) / width + EPS)


def _sigmoid(x):
    return 0.5 * jnp.tanh(0.5 * x) + 0.5


def _rmsnorm_kernel(x_ref, g_ref, o_ref):
    x = x_ref[...]
    o_ref[...] = (_rms(x, x.shape[-1]) * g_ref[...]).astype(o_ref.dtype)


def rmsnorm_rows(x, g, tm=256):
    m, d = x.shape
    return pl.pallas_call(
        _rmsnorm_kernel,
        grid=(m // tm,),
        in_specs=[pl.BlockSpec((tm, d), lambda i: (i, 0)),
                  pl.BlockSpec((1, d), lambda i: (0, 0))],
        out_specs=pl.BlockSpec((tm, d), lambda i: (i, 0)),
        out_shape=jax.ShapeDtypeStruct((m, d), BF16),
        compiler_params=_params("parallel"),
        name="rmsnorm_rows",
    )(x, g.reshape(1, d))


CAST_ROWS = 512


def _stage_weight(dst_ref, w_ref, wx_ref=None, shift=0):
    k, tn = dst_ref.shape
    rows = min(CAST_ROWS, k)

    def body(r, carry):
        sl = pl.ds(pl.multiple_of(r * rows, rows), rows)
        w = w_ref[sl, :]
        if shift:
            w = jnp.concatenate([w, wx_ref[sl, :]], axis=1)[:, shift:shift + tn]
        dst_ref[sl, :] = w.astype(BF16)
        return carry

    lax.fori_loop(0, k // rows, body, 0)


def _wmm_kernel(*refs, shift, residual):
    a_ref, w_ref = refs[0], refs[1]
    wx_ref = refs[2] if shift else None
    r_ref = refs[2 + bool(shift)] if residual else None
    o_ref, wbf_ref = refs[-2], refs[-1]

    @pl.when(pl.program_id(1) == 0)
    def _():
        _stage_weight(wbf_ref, w_ref, wx_ref, shift)

    d = _dot(a_ref[...], wbf_ref[...])
    if residual:
        d = r_ref[...] + d
    o_ref[...] = d.astype(o_ref.dtype)


def weight_matmul(a, w, layer, out_dtype, tm, tn, *, col0=0, n=None, res=None):
    m, k = a.shape
    n = w.shape[2] - col0 if n is None else n
    base, shift = divmod(col0, tn)
    lane_blocks = tn // HEAD_DIM
    in_specs = [pl.BlockSpec((tm, k), lambda j, i: (i, 0)),
                pl.BlockSpec((None, k, tn), lambda j, i: (layer, 0, base + j))]
    args = [a, w]
    if shift:
        in_specs.append(pl.BlockSpec((None, k, HEAD_DIM), lambda j, i: (layer, 0, (base + j + 1) * lane_blocks)))
        args.append(w)
    if res is not None:
        in_specs.append(pl.BlockSpec((tm, tn), lambda j, i: (i, j)))
        args.append(res)
    return pl.pallas_call(
        functools.partial(_wmm_kernel, shift=shift, residual=res is not None),
        grid=(n // tn, m // tm),
        in_specs=in_specs,
        out_specs=pl.BlockSpec((tm, tn), lambda j, i: (i, j)),
        out_shape=jax.ShapeDtypeStruct((m, n), out_dtype),
        scratch_shapes=[pltpu.VMEM((k, tn), BF16)],
        compiler_params=_params("arbitrary", "arbitrary"),
        name="weight_matmul",
    )(*args)


def _headnorm_kernel(x_ref, g_ref, o_ref):
    g = g_ref[0]
    for h in range(x_ref.shape[1] // HEAD_DIM):
        sl = slice(h * HEAD_DIM, (h + 1) * HEAD_DIM)
        x = x_ref[:, sl].astype(F32)
        o_ref[:, sl] = (_rms(x, HEAD_DIM) * g).astype(o_ref.dtype)


def headnorm_qk(proj, gains, width, tm=512):
    m = proj.shape[0]
    return pl.pallas_call(
        _headnorm_kernel,
        grid=(m // tm, 2),
        in_specs=[pl.BlockSpec((tm, width), lambda i, j: (i, j)),
                  pl.BlockSpec((1, 1, HEAD_DIM), lambda i, j: (j, 0, 0))],
        out_specs=pl.BlockSpec((tm, width), lambda i, j: (i, j)),
        out_shape=jax.ShapeDtypeStruct((m, 2 * width), BF16),
        compiler_params=_params("parallel", "arbitrary"),
        name="headnorm_qk",
    )(proj, gains)


def _kv_block(k_ref, v_ref, j, blk):
    start = pl.multiple_of(j * blk, blk)
    return k_ref[pl.ds(start, blk), :], v_ref[pl.ds(start, blk), :]


def _sb_block(q, ks, vs, carry, acc, from_here, scale, strict):
    z = _dot_nt(q, ks) * scale
    log_1m_beta = -(jnp.maximum(z, 0.0) + jnp.log(1.0 + jnp.exp(-jnp.abs(z))))
    if strict is not None:
        log_1m_beta = jnp.where(strict, log_1m_beta, 0.0)
    hi = log_1m_beta.astype(BF16)
    lo = (log_1m_beta - hi.astype(F32)).astype(BF16)
    incl = _dot(jnp.concatenate([hi, lo], axis=1), from_here)
    w = jnp.exp(z + incl + carry)
    if strict is not None:
        w = jnp.where(strict, w, 0.0)
    acc = acc + _dot(w.astype(BF16), vs)
    carry = carry + jnp.sum(log_1m_beta, axis=1, keepdims=True)
    return carry, acc


def _sb_attn_kernel(q_ref, k_ref, v_ref, o_ref, *, blk, scale):
    qi = pl.program_id(2)
    qa = q_ref[:blk, :]
    qb = q_ref[blk:, :]
    row = lax.broadcasted_iota(jnp.int32, (blk, blk), 0)
    col = lax.broadcasted_iota(jnp.int32, (blk, blk), 1)
    strict = col < row
    tri = (row >= col).astype(BF16)
    from_here = jnp.concatenate([tri, tri], axis=0)
    block = functools.partial(_sb_block, from_here=from_here, scale=scale)
    zero_c = jnp.zeros((blk, 1), F32)
    zero_a = jnp.zeros((blk, HEAD_DIM), F32)

    ks, vs = _kv_block(k_ref, v_ref, 2 * qi + 1, blk)
    cb, ab = block(qb, ks, vs, zero_c, zero_a, strict=strict)
    ks, vs = _kv_block(k_ref, v_ref, 2 * qi, blk)
    ca, aa = block(qa, ks, vs, zero_c, zero_a, strict=strict)
    cb, ab = block(qb, ks, vs, cb, ab, strict=None)

    def body(t, c):
        ca, aa, cb, ab = c
        for u in range(2):
            ks, vs = _kv_block(k_ref, v_ref, 2 * (qi - t) - 1 - u, blk)
            ca, aa = block(qa, ks, vs, ca, aa, strict=None)
            cb, ab = block(qb, ks, vs, cb, ab, strict=None)
        return ca, aa, cb, ab

    ca, aa, cb, ab = lax.fori_loop(0, qi, body, (ca, aa, cb, ab))
    o_ref[:blk, :] = aa.astype(o_ref.dtype)
    o_ref[blk:, :] = ab.astype(o_ref.dtype)


def sb_attention(qk, proj, batch, seq, blk=256):
    width = N_HEADS * HEAD_DIM
    nq = seq // (2 * blk)
    kern = functools.partial(_sb_attn_kernel, blk=blk, scale=HEAD_DIM ** -0.5)
    return pl.pallas_call(
        kern,
        grid=(batch, N_HEADS, nq),
        in_specs=[pl.BlockSpec((2 * blk, HEAD_DIM), lambda b, h, i: (b * nq + i, h)),
                  pl.BlockSpec((seq, HEAD_DIM), lambda b, h, i: (b, N_HEADS + h)),
                  pl.BlockSpec((seq, HEAD_DIM), lambda b, h, i: (b, 2 * N_HEADS + h))],
        out_specs=pl.BlockSpec((2 * blk, HEAD_DIM), lambda b, h, i: (b * nq + i, h)),
        out_shape=jax.ShapeDtypeStruct((batch * seq, width), BF16),
        compiler_params=_params("parallel", "parallel", "arbitrary"),
        name="sb_attention",
    )(qk, qk, proj)


def _rope_lanes(x, cos_t, sin_t):
    lane = lax.broadcasted_iota(jnp.int32, x.shape, 1)
    half = MLA_ROPE // 2
    partner = jnp.where(lane < half, pltpu.roll(x, HEAD_DIM - half, 1), pltpu.roll(x, half, 1))
    return x * cos_t + partner * sin_t


def _mla_prep_kernel(p_ref, kr_ref, qa_ref, kva_ref, wq_ref, wkn_ref, wv_ref, gq_ref, gk_ref,
                     cos_ref, sin_ref, q_out, k_out, v_out, *, q_lora):
    cos_t = cos_ref[...]
    sin_t = sin_ref[...]
    gq = gq_ref[...]
    gk = gk_ref[...]
    cq = p_ref[:, :q_lora].astype(F32)
    cqn = (_rms(cq, q_lora) * qa_ref[...]).astype(BF16)
    qraw = _dot(cqn, wq_ref[...])
    for h in range(N_HEADS):
        blk = qraw[:, h * MLA_PAD:(h + 1) * MLA_PAD]
        y = _rms(blk, MLA_QK) * gq
        q_out[:, h * MLA_PAD:h * MLA_PAD + HEAD_DIM] = y[:, :HEAD_DIM].astype(q_out.dtype)
        q_out[:, h * MLA_PAD + HEAD_DIM:(h + 1) * MLA_PAD] = _rope_lanes(
            y[:, HEAD_DIM:], cos_t, sin_t).astype(q_out.dtype)

    ckv = p_ref[:, q_lora:].astype(F32)
    ckvn = (_rms(ckv, ckv.shape[-1]) * kva_ref[...]).astype(BF16)
    kn = _dot(ckvn, wkn_ref[...])
    v_out[...] = _dot(ckvn, wv_ref[...]).astype(v_out.dtype)
    kr = kr_ref[...]
    lane = lax.broadcasted_iota(jnp.int32, kr.shape, 1)
    kr = jnp.where(lane < MLA_ROPE, kr, 0.0)
    kr_ss = jnp.sum(kr * kr, axis=-1, keepdims=True)
    for h in range(N_HEADS):
        kb = kn[:, h * HEAD_DIM:(h + 1) * HEAD_DIM]
        inv = lax.rsqrt((jnp.sum(kb * kb, axis=-1, keepdims=True) + kr_ss) / MLA_QK + EPS)
        k_out[:, h * MLA_PAD:h * MLA_PAD + HEAD_DIM] = (kb * inv * gk[:, :HEAD_DIM]).astype(k_out.dtype)
        k_out[:, h * MLA_PAD + HEAD_DIM:(h + 1) * MLA_PAD] = _rope_lanes(
            kr * inv * gk[:, HEAD_DIM:], cos_t, sin_t).astype(k_out.dtype)


def mla_prep(proj, kr, qa, kva, wq, wkn, wv, gq, gk, cos_t, sin_t, seq, q_lora, kv_lora, tm=512):
    m = proj.shape[0]
    lat = q_lora + kv_lora
    lat_blk = (3 * N_HEADS * HEAD_DIM) // lat
    assert lat_blk * lat == 3 * N_HEADS * HEAD_DIM
    ns = seq // tm
    full = lambda i: (0, 0)
    kern = functools.partial(_mla_prep_kernel, q_lora=q_lora)
    return pl.pallas_call(
        kern,
        grid=(m // tm,),
        in_specs=[pl.BlockSpec((tm, lat), lambda i: (i, lat_blk)),
                  pl.BlockSpec((tm, HEAD_DIM), lambda i: (i, 0)),
                  pl.BlockSpec((1, q_lora), full),
                  pl.BlockSpec((1, kv_lora), full),
                  pl.BlockSpec(wq.shape, full),
                  pl.BlockSpec(wkn.shape, full),
                  pl.BlockSpec(wv.shape, full),
                  pl.BlockSpec((1, MLA_PAD), full),
                  pl.BlockSpec((1, MLA_PAD), full),
                  pl.BlockSpec((tm, HEAD_DIM), lambda i: (i % ns, 0)),
                  pl.BlockSpec((tm, HEAD_DIM), lambda i: (i % ns, 0))],
        out_specs=[pl.BlockSpec((tm, N_HEADS * MLA_PAD), lambda i: (i, 0)),
                   pl.BlockSpec((tm, N_HEADS * MLA_PAD), lambda i: (i, 0)),
                   pl.BlockSpec((tm, N_HEADS * HEAD_DIM), lambda i: (i, 0))],
        out_shape=[jax.ShapeDtypeStruct((m, N_HEADS * MLA_PAD), BF16),
                   jax.ShapeDtypeStruct((m, N_HEADS * MLA_PAD), BF16),
                   jax.ShapeDtypeStruct((m, N_HEADS * HEAD_DIM), BF16)],
        compiler_params=_params("parallel"),
        name="mla_prep",
    )(proj, kr, qa, kva, wq, wkn, wv, gq, gk, cos_t, sin_t)


def _softmax_block(q, ks, vs, state, scale, causal):
    m, l, acc = state
    z = _dot_nt(q, ks)
    if causal is not None:
        z = jnp.where(causal, z, MASK_NEG)
    m_new = jnp.maximum(m, jnp.max(z, axis=1, keepdims=True))
    p = jnp.exp((z - m_new) * scale)
    alpha = jnp.exp((m - m_new) * scale)
    l = alpha * l + jnp.sum(p, axis=1, keepdims=True)
    acc = alpha * acc + _dot(p.astype(BF16), vs)
    return m_new, l, acc


def _mla_attn_kernel(q_ref, k_ref, v_ref, o_ref, *, blk, scale):
    qi = pl.program_id(2)
    qa = q_ref[:blk, :]
    qb = q_ref[blk:, :]
    row = lax.broadcasted_iota(jnp.int32, (blk, blk), 0)
    col = lax.broadcasted_iota(jnp.int32, (blk, blk), 1)
    causal = col <= row
    block = functools.partial(_softmax_block, scale=scale)
    init = (jnp.full((blk, 1), MASK_NEG, F32), jnp.zeros((blk, 1), F32), jnp.zeros((blk, HEAD_DIM), F32))

    def body(t, c):
        sa, sb = c
        for u in range(2):
            ks, vs = _kv_block(k_ref, v_ref, 2 * t + u, blk)
            sa = block(qa, ks, vs, sa, causal=None)
            sb = block(qb, ks, vs, sb, causal=None)
        return sa, sb

    sa, sb = lax.fori_loop(0, qi, body, (init, init))
    ks, vs = _kv_block(k_ref, v_ref, 2 * qi, blk)
    sa = block(qa, ks, vs, sa, causal=causal)
    sb = block(qb, ks, vs, sb, causal=None)
    ks, vs = _kv_block(k_ref, v_ref, 2 * qi + 1, blk)
    sb = block(qb, ks, vs, sb, causal=causal)
    o_ref[:blk, :] = (sa[2] / sa[1]).astype(o_ref.dtype)
    o_ref[blk:, :] = (sb[2] / sb[1]).astype(o_ref.dtype)


def mla_attention(q, k, v, batch, seq, blk=256):
    nq = seq // (2 * blk)
    kern = functools.partial(_mla_attn_kernel, blk=blk, scale=MLA_QK ** -0.5)
    return pl.pallas_call(
        kern,
        grid=(batch, N_HEADS, nq),
        in_specs=[pl.BlockSpec((2 * blk, MLA_PAD), lambda b, h, i: (b * nq + i, h)),
                  pl.BlockSpec((seq, MLA_PAD), lambda b, h, i: (b, h)),
                  pl.BlockSpec((seq, HEAD_DIM), lambda b, h, i: (b, h))],
        out_specs=pl.BlockSpec((2 * blk, HEAD_DIM), lambda b, h, i: (b * nq + i, h)),
        out_shape=jax.ShapeDtypeStruct((batch * seq, N_HEADS * HEAD_DIM), BF16),
        compiler_params=_params("parallel", "parallel", "arbitrary"),
        name="mla_attention",
    )(q, k, v)


def _seg_cumsum(x, seg, row):
    pos = row & (seg - 1)
    s = 1
    while s < seg:
        x = x + jnp.where(pos >= s, pltpu.roll(x, s, 0), 0.0)
        s *= 2
    return x


def _seg_rev_cumsum_excl(x, seg, row):
    pos = row & (seg - 1)
    n = x.shape[0]
    y = x
    s = 1
    while s < seg:
        y = y + jnp.where(pos < seg - s, pltpu.roll(y, n - s, 0), 0.0)
        s *= 2
    return y - x


def _rebase_halves(x, half, second):
    pieces = []
    for s in range(0, x.shape[0], 2 * half):
        pivot = x[s + half - 1:s + half]
        lo, hi = x[s:s + half], x[s + half:s + 2 * half]
        pieces += [lo, hi - pivot] if second else [lo - pivot, hi]
    return jnp.concatenate(pieces, axis=0)


SUBLANES = 8


def _hgrn2_kernel(lb_ref, q_ref, z_ref, i_ref, g_ref, gn_ref, o_ref, st_ref, *, rows, layer):
    @pl.when(pl.program_id(2) == 0)
    def _():
        st_ref[...] = jnp.zeros_like(st_ref)

    lbp = lb_ref[...]
    e = jnp.exp(lbp - jnp.max(lbp, axis=0, keepdims=True))
    p = e / jnp.sum(e, axis=0, keepdims=True)
    lb = jnp.sum(p[:layer + 1], axis=0, keepdims=True) - p[0:1]

    z = z_ref[...]
    q = q_ref[...]
    v = i_ref[...]
    log_sig = jnp.minimum(z, 0.0) - jnp.log(1.0 + jnp.exp(-jnp.abs(z)))
    a = jnp.log(jnp.maximum(lb, LB_FLOOR))
    c = jnp.log1p(-lb) + log_sig
    log_f = jnp.maximum(a, c) + jnp.log(1.0 + jnp.exp(-jnp.abs(a - c)))
    k = 1.0 - jnp.exp(log_f)

    row = lax.broadcasted_iota(jnp.int32, (rows, HEAD_DIM), 0)
    fwd = {HG_CHUNK: _seg_cumsum(log_f, HG_CHUNK, row)}
    rev = {HG_CHUNK: _seg_rev_cumsum_excl(log_f, HG_CHUNK, row)}
    half = HG_CHUNK // 2
    while half >= 1:
        if half >= SUBLANES:
            fwd[half] = _rebase_halves(fwd[2 * half], half, second=True)
            rev[half] = _rebase_halves(rev[2 * half], half, second=False)
        else:
            fwd[half] = _seg_cumsum(log_f, half, row)
            rev[half] = _seg_rev_cumsum_excl(log_f, half, row)
        half //= 2
    b = fwd[HG_CHUNK]
    q_in = (q * jnp.exp(b)).astype(BF16)
    k_out = (k * jnp.exp(rev[HG_CHUNK])).astype(BF16)
    v16 = v.astype(BF16)
    diag = jnp.sum(q * k, axis=1, keepdims=True)

    levels = []
    half = 1
    while half < HG_CHUNK:
        second = (row & (2 * half - 1)) >= half
        ql = jnp.where(second, q * jnp.exp(fwd[half]), 0.0).astype(BF16)
        kl = jnp.where(second, 0.0, k * jnp.exp(rev[half])).astype(BF16)
        levels.append((2 * half, ql, kl))
        half *= 2

    ct = lax.broadcasted_iota(jnp.int32, (HG_CHUNK, HG_CHUNK), 0)
    cs = lax.broadcasted_iota(jnp.int32, (HG_CHUNK, HG_CHUNK), 1)
    st = st_ref[...]
    outs = []
    for ci in range(rows // HG_CHUNK):
        sl = slice(ci * HG_CHUNK, (ci + 1) * HG_CHUNK)
        attn = jnp.zeros((HG_CHUNK, HG_CHUNK), F32)
        for w, ql, kl in levels:
            part = _dot_nt(ql[sl], kl[sl])
            if w < HG_CHUNK:
                part = jnp.where((ct & -w) == (cs & -w), part, 0.0)
            attn = attn + part
        o = _dot(attn.astype(BF16), v16[sl]) + diag[sl] * v[sl] + _dot_nt(q_in[sl], st.astype(BF16))
        outs.append(o)
        b_last = b[(ci + 1) * HG_CHUNK - 1:(ci + 1) * HG_CHUNK, :]
        st = st * jnp.exp(b_last) + _dot_tn(v16[sl], k_out[sl])
    st_ref[...] = st

    o = jnp.concatenate(outs, axis=0)
    g = g_ref[...]
    o_ref[...] = (_rms(o, HEAD_DIM) * gn_ref[...] * (g * _sigmoid(g))).astype(o_ref.dtype)


def hgrn2(proj, lower_bounds, out_norm, layer, batch, seq, rows=256):
    depth = lower_bounds.shape[0]
    nr = seq // rows
    kern = functools.partial(_hgrn2_kernel, rows=rows, layer=layer)

    def col(c):
        return pl.BlockSpec((rows, HEAD_DIM), lambda b, h, r: (b * nr + r, c * N_HEADS + h))

    return pl.pallas_call(
        kern,
        grid=(batch, N_HEADS, nr),
        in_specs=[pl.BlockSpec((depth, HEAD_DIM), lambda b, h, r: (0, h)),
                  col(0), col(1), col(2), col(3),
                  pl.BlockSpec((1, HEAD_DIM), lambda b, h, r: (0, 0))],
        out_specs=pl.BlockSpec((rows, HEAD_DIM), lambda b, h, r: (b * nr + r, h)),
        out_shape=jax.ShapeDtypeStruct((batch * seq, N_HEADS * HEAD_DIM), BF16),
        scratch_shapes=[pltpu.VMEM((HEAD_DIM, HEAD_DIM), F32)],
        compiler_params=_params("parallel", "parallel", "arbitrary"),
        name="hgrn2",
    )(lower_bounds, proj, proj, proj, proj, out_norm.reshape(1, HEAD_DIM))


def _merge_kernel(a0, a1, a2, w0, w1, w2, g0, g1, g2, o_ref, wbf_ref):
    @pl.when(pl.program_id(1) == 0)
    def _():
        for n, w_ref in enumerate((w0, w1, w2)):
            _stage_weight(wbf_ref.at[n], w_ref)

    acc = _sigmoid(g0[...].astype(F32)) * _dot(a0[...], wbf_ref[0])
    acc = acc + _sigmoid(g1[...].astype(F32)) * _dot(a1[...], wbf_ref[1])
    acc = acc + _sigmoid(g2[...].astype(F32)) * _dot(a2[...], wbf_ref[2])
    o_ref[...] = acc.astype(o_ref.dtype)


def gated_merge(o_sb, o_mla, o_hg, w_sb, w_mla, w_hg, layer, gate_logits, tm=1024, tn=512):
    m, k = o_sb.shape
    n = w_sb.shape[2]
    nb = n // tn
    a_spec = pl.BlockSpec((tm, k), lambda j, i: (i, 0))
    w_spec = pl.BlockSpec((None, k, tn), lambda j, i: (layer, 0, j))

    def g_spec(branch):
        return pl.BlockSpec((tm, tn), lambda j, i: (i, branch * nb + j))

    return pl.pallas_call(
        _merge_kernel,
        grid=(nb, m // tm),
        in_specs=[a_spec, a_spec, a_spec, w_spec, w_spec, w_spec, g_spec(0), g_spec(1), g_spec(2)],
        out_specs=pl.BlockSpec((tm, tn), lambda j, i: (i, j)),
        out_shape=jax.ShapeDtypeStruct((m, n), BF16),
        scratch_shapes=[pltpu.VMEM((3, k, tn), BF16)],
        compiler_params=_params("arbitrary", "arbitrary"),
        name="gated_merge",
    )(o_sb, o_mla, o_hg, w_sb, w_mla, w_hg, gate_logits, gate_logits, gate_logits)


HALO = 16


def _up_conv_glu_kernel(a_ref, ah_ref, wg_ref, wv_ref, cg_ref, cv_ref, o_ref, wbf_ref, *, tiles_per_seq):
    @pl.when(pl.program_id(1) == 0)
    def _():
        _stage_weight(wbf_ref.at[0], wg_ref)
        _stage_weight(wbf_ref.at[1], wv_ref)

    a = a_ref[...]
    ah = ah_ref[...]
    seq_start = (pl.program_id(1) % tiles_per_seq) == 0
    row = lax.broadcasted_iota(jnp.int32, o_ref.shape, 0)

    def conv(slot, c_ref):
        w = wbf_ref[slot]
        u = _dot(a, w)
        halo = jnp.where(seq_start, 0.0, _dot(ah, w))
        prev1 = halo[HALO - 1:HALO]
        prev2 = halo[HALO - 2:HALO - 1]
        u1 = jnp.where(row == 0, prev1, pltpu.roll(u, 1, 0))
        u2 = jnp.where(row == 0, prev2, jnp.where(row == 1, prev1, pltpu.roll(u, 2, 0)))
        c = c_ref[...]
        return u2 * c[0:1] + u1 * c[1:2] + u * c[2:3]

    gate = conv(0, cg_ref)
    val = conv(1, cv_ref)
    o_ref[...] = (gate * _sigmoid(gate) * val).astype(o_ref.dtype)


def up_conv_glu(h, w_up, conv_w, layer, seq, tm=1024, tn=256):
    m, k = h.shape
    d_ff = w_up.shape[2] // 2
    nb = d_ff // tn
    kern = functools.partial(_up_conv_glu_kernel, tiles_per_seq=seq // tm)
    halo_blocks = tm // HALO
    return pl.pallas_call(
        kern,
        grid=(nb, m // tm),
        in_specs=[pl.BlockSpec((tm, k), lambda j, i: (i, 0)),
                  pl.BlockSpec((HALO, k), lambda j, i: (jnp.maximum(i * halo_blocks - 1, 0), 0)),
                  pl.BlockSpec((None, k, tn), lambda j, i: (layer, 0, j)),
                  pl.BlockSpec((None, k, tn), lambda j, i: (layer, 0, nb + j)),
                  pl.BlockSpec((None, conv_w.shape[1], tn), lambda j, i: (layer, 0, j)),
                  pl.BlockSpec((None, conv_w.shape[1], tn), lambda j, i: (layer, 0, nb + j))],
        out_specs=pl.BlockSpec((tm, tn), lambda j, i: (i, j)),
        out_shape=jax.ShapeDtypeStruct((m, d_ff), BF16),
        scratch_shapes=[pltpu.VMEM((2, k, tn), BF16)],
        compiler_params=_params("arbitrary", "arbitrary"),
        name="up_conv_glu",
    )(h, h, w_up, w_up, conv_w, conv_w)


def _rope_lane_tables(seq):
    pos = jnp.arange(seq, dtype=F32)
    inv = ROPE_THETA ** (-jnp.arange(0, MLA_ROPE, 2, dtype=F32) / MLA_ROPE)
    ang = pos[:, None] * inv[None, :]
    cos, sin = jnp.cos(ang), jnp.sin(ang)
    zeros = jnp.zeros((seq, HEAD_DIM - MLA_ROPE), F32)
    return (jnp.concatenate([cos, cos, zeros], axis=1),
            jnp.concatenate([-sin, sin, zeros], axis=1))


def _pad_head_gain(g):
    return jnp.pad(g, (0, MLA_PAD - MLA_QK)).reshape(1, MLA_PAD)


def kernel(x, lower_bounds, attn_norm, w_in, sb_q_norm, sb_k_norm, mla_q_a_norm, mla_kv_a_norm, mla_w_q_b, mla_w_kv_b, mla_q_norm, mla_k_norm, hg_out_norm, w_branch_sb, w_branch_mla, w_branch_hg, w_out, ffn_norm, w_up, ffn_conv, w_down):
    batch, seq, d_model = x.shape
    depth = w_in.shape[0]
    width = N_HEADS * HEAD_DIM
    q_lora = mla_w_q_b.shape[1]
    kv_lora = mla_w_kv_b.shape[1]
    c_attn = 3 * width + q_lora + kv_lora
    c_hg = c_attn + MLA_ROPE
    c_gate = c_hg + 4 * width

    cos_t, sin_t = _rope_lane_tables(seq)
    xf = x.reshape(batch * seq, d_model)
    for l in range(depth):
        wq = jnp.pad(mla_w_q_b[l], ((0, 0), (0, 0), (0, MLA_PAD - MLA_QK))).reshape(q_lora, N_HEADS * MLA_PAD)
        wkn = mla_w_kv_b[l][:, :, :HEAD_DIM].reshape(kv_lora, width)
        wv = mla_w_kv_b[l][:, :, HEAD_DIM:].reshape(kv_lora, width)

        h = rmsnorm_rows(xf, attn_norm[l])
        p_attn = weight_matmul(h, w_in, l, BF16, 1024, 512, n=c_attn)
        p_kr = weight_matmul(h, w_in, l, F32, 1024, HEAD_DIM, col0=c_attn, n=HEAD_DIM)
        p_hg = weight_matmul(h, w_in, l, F32, 1024, 512, col0=c_hg, n=c_gate - c_hg)
        p_gate = weight_matmul(h, w_in, l, BF16, 1024, 512, col0=c_gate)

        sb_qk = headnorm_qk(p_attn, jnp.stack([sb_q_norm[l], sb_k_norm[l]]).reshape(2, 1, HEAD_DIM), width)
        o_sb = sb_attention(sb_qk, p_attn, batch, seq)

        mq, mk, mv = mla_prep(p_attn, p_kr, mla_q_a_norm[l].reshape(1, q_lora),
                              mla_kv_a_norm[l].reshape(1, kv_lora), wq.astype(BF16), wkn.astype(BF16),
                              wv.astype(BF16), _pad_head_gain(mla_q_norm[l]), _pad_head_gain(mla_k_norm[l]),
                              cos_t, sin_t, seq, q_lora, kv_lora)
        o_mla = mla_attention(mq, mk, mv, batch, seq)

        o_hg = hgrn2(p_hg, lower_bounds, hg_out_norm[l], l, batch, seq)

        merged = gated_merge(o_sb, o_mla, o_hg, w_branch_sb, w_branch_mla, w_branch_hg, l, p_gate)
        xf = weight_matmul(merged, w_out, l, F32, 1024, 512, res=xf)

        h2 = rmsnorm_rows(xf, ffn_norm[l])
        act = up_conv_glu(h2, w_up, ffn_conv, l, seq)
        xf = weight_matmul(act, w_down, l, F32, 512, 256, res=xf)
    return xf.reshape(batch, seq, d_model)
```

```python
import functools

import jax
import jax.numpy as jnp
from jax import lax
from jax.experimental import pallas as pl
from jax.experimental.pallas import tpu as pltpu

F32 = jnp.float32
BF16 = jnp.bfloat16

SUBLANES = 8
HEAD_DIM = 128
N_HEADS = 8
MLA_ROPE = 64
MLA_QK = HEAD_DIM + MLA_ROPE
MLA_PAD = 256
HG_CHUNK = 64
ROPE_THETA = 10000.0
EPS = 1e-6
MASK_NEG = -1e30
LB_FLOOR = 1e-30

VMEM_LIMIT_BYTES = 56 * 1024 * 1024


def _params(*sem):
    return pltpu.CompilerParams(dimension_semantics=sem, vmem_limit_bytes=VMEM_LIMIT_BYTES)


def _dot(a, b):
    return jnp.dot(a, b, preferred_element_type=F32)


def _dot_nt(a, b):
    return lax.dot_general(a, b, (((1,), (1,)), ((), ())), preferred_element_type=F32)


def _dot_tn(a, b):
    return lax.dot_general(a, b, (((0,), (0,)), ((), ())), preferred_element_type=F32)


def _rms(x, width):
    return x * lax.rsqrt(jnp.sum(x * x, axis=-1, keepdims=True) / width + EPS)


def _sigmoid(x):
    return 0.5 * jnp.tanh(0.5 * x) + 0.5


def _rmsnorm_kernel(x_ref, g_ref, o_ref):
    x = x_ref[...]
    o_ref[...] = (_rms(x, x.shape[-1]) * g_ref[...]).astype(o_ref.dtype)


def rmsnorm_rows(x, g, tm=256):
    m, d = x.shape
    return pl.pallas_call(
        _rmsnorm_kernel,
        grid=(m // tm,),
        in_specs=[pl.BlockSpec((tm, d), lambda i: (i, 0)),
                  pl.BlockSpec((1, d), lambda i: (0, 0))],
        out_specs=pl.BlockSpec((tm, d), lambda i: (i, 0)),
        out_shape=jax.ShapeDtypeStruct((m, d), BF16),
        compiler_params=_params("parallel"),
        name="rmsnorm_rows",
    )(x, g.reshape(1, d))


CAST_ELEMS = 512 * 512


def _stage_weight(dst_ref, w_ref):
    nrows, ncols = dst_ref.shape
    rows = min(nrows, CAST_ELEMS // ncols)

    def body(r, carry):
        sl = pl.ds(pl.multiple_of(r * rows, rows), rows)
        dst_ref[sl, :] = w_ref[sl, :].astype(BF16)
        return carry

    lax.fori_loop(0, nrows // rows, body, 0)


def _wmm_kernel(*refs, transposed, residual):
    a_ref, w_ref = refs[0], refs[1]
    r_ref = refs[2] if residual else None
    o_ref, wbf_ref = refs[-2], refs[-1]

    @pl.when(pl.program_id(1) == 0)
    def _():
        _stage_weight(wbf_ref, w_ref.at[0] if transposed else w_ref)

    d = (_dot_nt if transposed else _dot)(a_ref[...], wbf_ref[...])
    if residual:
        d = r_ref[...] + d
    o_ref[...] = d.astype(o_ref.dtype)


def weight_matmul(a, w, layer, out_dtype, tm, tn, *, transposed=False, col0=0, n=None, k_part=(0, 1), res=None):
    m = a.shape[0]
    kp, parts = k_part
    k = a.shape[1] // parts
    if transposed:
        assert parts == 1 and w.shape[2] == k
        n = w.shape[1] - col0 if n is None else n
        w_spec = pl.BlockSpec((pl.Element(1), pl.Element(tn), pl.Element(k)),
                              lambda j, i: (layer, pl.multiple_of(col0 + j * tn, SUBLANES), 0))
        staged = (tn, k)
    else:
        assert col0 % tn == 0 and w.shape[1] == k * parts
        n = w.shape[2] - col0 if n is None else n
        w_spec = pl.BlockSpec((None, k, tn), lambda j, i: (layer, kp, col0 // tn + j))
        staged = (k, tn)
    in_specs = [pl.BlockSpec((tm, k), lambda j, i: (i, kp)), w_spec]
    args = [a, w]
    if res is not None:
        in_specs.append(pl.BlockSpec((tm, tn), lambda j, i: (i, j)))
        args.append(res)
    return pl.pallas_call(
        functools.partial(_wmm_kernel, transposed=transposed, residual=res is not None),
        grid=(n // tn, m // tm),
        in_specs=in_specs,
        out_specs=pl.BlockSpec((tm, tn), lambda j, i: (i, j)),
        out_shape=jax.ShapeDtypeStruct((m, n), out_dtype),
        scratch_shapes=[pltpu.VMEM(staged, BF16)],
        compiler_params=_params("arbitrary", "arbitrary"),
        name="weight_matmul",
    )(*args)


def _headnorm_kernel(x_ref, g_ref, o_ref):
    g = g_ref[0]
    for h in range(x_ref.shape[1] // HEAD_DIM):
        sl = slice(h * HEAD_DIM, (h + 1) * HEAD_DIM)
        x = x_ref[:, sl].astype(F32)
        o_ref[:, sl] = (_rms(x, HEAD_DIM) * g).astype(o_ref.dtype)


def headnorm_qk(proj, gains, width, tm=512):
    m = proj.shape[0]
    return pl.pallas_call(
        _headnorm_kernel,
        grid=(m // tm, 2),
        in_specs=[pl.BlockSpec((tm, width), lambda i, j: (i, j)),
                  pl.BlockSpec((1, 1, HEAD_DIM), lambda i, j: (j, 0, 0))],
        out_specs=pl.BlockSpec((tm, width), lambda i, j: (i, j)),
        out_shape=jax.ShapeDtypeStruct((m, 2 * width), BF16),
        compiler_params=_params("parallel", "arbitrary"),
        name="headnorm_qk",
    )(proj, gains)


def _kv_block(k_ref, v_ref, j, blk):
    start = pl.multiple_of(j * blk, blk)
    return k_ref[pl.ds(start, blk), :], v_ref[pl.ds(start, blk), :]


def _sb_block(q, ks, vs, carry, acc, from_here, scale, strict):
    z = _dot_nt(q, ks) * scale
    log_1m_beta = -(jnp.maximum(z, 0.0) + jnp.log(1.0 + jnp.exp(-jnp.abs(z))))
    if strict is not None:
        log_1m_beta = jnp.where(strict, log_1m_beta, 0.0)
    hi = log_1m_beta.astype(BF16)
    lo = (log_1m_beta - hi.astype(F32)).astype(BF16)
    incl = _dot(jnp.concatenate([hi, lo], axis=1), from_here)
    w = jnp.exp(z + incl + carry)
    if strict is not None:
        w = jnp.where(strict, w, 0.0)
    acc = acc + _dot(w.astype(BF16), vs)
    carry = carry + jnp.sum(log_1m_beta, axis=1, keepdims=True)
    return carry, acc


def _sb_attn_kernel(q_ref, k_ref, v_ref, o_ref, *, blk, scale):
    qi = pl.program_id(2)
    qa = q_ref[:blk, :]
    qb = q_ref[blk:, :]
    row = lax.broadcasted_iota(jnp.int32, (blk, blk), 0)
    col = lax.broadcasted_iota(jnp.int32, (blk, blk), 1)
    strict = col < row
    tri = (row >= col).astype(BF16)
    from_here = jnp.concatenate([tri, tri], axis=0)
    block = functools.partial(_sb_block, from_here=from_here, scale=scale)
    zero_c = jnp.zeros((blk, 1), F32)
    zero_a = jnp.zeros((blk, HEAD_DIM), F32)

    ks, vs = _kv_block(k_ref, v_ref, 2 * qi + 1, blk)
    cb, ab = block(qb, ks, vs, zero_c, zero_a, strict=strict)
    ks, vs = _kv_block(k_ref, v_ref, 2 * qi, blk)
    ca, aa = block(qa, ks, vs, zero_c, zero_a, strict=strict)
    cb, ab = block(qb, ks, vs, cb, ab, strict=None)

    def body(t, c):
        ca, aa, cb, ab = c
        for u in range(2):
            ks, vs = _kv_block(k_ref, v_ref, 2 * (qi - t) - 1 - u, blk)
            ca, aa = block(qa, ks, vs, ca, aa, strict=None)
            cb, ab = block(qb, ks, vs, cb, ab, strict=None)
        return ca, aa, cb, ab

    ca, aa, cb, ab = lax.fori_loop(0, qi, body, (ca, aa, cb, ab))
    o_ref[:blk, :] = aa.astype(o_ref.dtype)
    o_ref[blk:, :] = ab.astype(o_ref.dtype)


def sb_attention(qk, proj, batch, seq, blk=256):
    width = N_HEADS * HEAD_DIM
    nq = seq // (2 * blk)
    kern = functools.partial(_sb_attn_kernel, blk=blk, scale=HEAD_DIM ** -0.5)
    return pl.pallas_call(
        kern,
        grid=(batch, N_HEADS, nq),
        in_specs=[pl.BlockSpec((2 * blk, HEAD_DIM), lambda b, h, i: (b * nq + i, h)),
                  pl.BlockSpec((seq, HEAD_DIM), lambda b, h, i: (b, N_HEADS + h)),
                  pl.BlockSpec((seq, HEAD_DIM), lambda b, h, i: (b, 2 * N_HEADS + h))],
        out_specs=pl.BlockSpec((2 * blk, HEAD_DIM), lambda b, h, i: (b * nq + i, h)),
        out_shape=jax.ShapeDtypeStruct((batch * seq, width), BF16),
        compiler_params=_params("parallel", "parallel", "arbitrary"),
        name="sb_attention",
    )(qk, qk, proj)


def _rope_lanes(x, cos_t, sin_t):
    lane = lax.broadcasted_iota(jnp.int32, x.shape, 1)
    half = MLA_ROPE // 2
    partner = jnp.where(lane < half, pltpu.roll(x, HEAD_DIM - half, 1), pltpu.roll(x, half, 1))
    return x * cos_t + partner * sin_t


def _mla_prep_kernel(p_ref, kr_ref, qa_ref, kva_ref, wq_ref, wkn_ref, wv_ref, gq_ref, gk_ref,
                     cos_ref, sin_ref, q_out, k_out, v_out, *, q_lora):
    cos_t = cos_ref[...]
    sin_t = sin_ref[...]
    gq = gq_ref[...]
    gk = gk_ref[...]
    cq = p_ref[:, :q_lora].astype(F32)
    cqn = (_rms(cq, q_lora) * qa_ref[...]).astype(BF16)
    qraw = _dot(cqn, wq_ref[...])
    for h in range(N_HEADS):
        blk = qraw[:, h * MLA_PAD:(h + 1) * MLA_PAD]
        y = _rms(blk, MLA_QK) * gq
        q_out[:, h * MLA_PAD:h * MLA_PAD + HEAD_DIM] = y[:, :HEAD_DIM].astype(q_out.dtype)
        q_out[:, h * MLA_PAD + HEAD_DIM:(h + 1) * MLA_PAD] = _rope_lanes(
            y[:, HEAD_DIM:], cos_t, sin_t).astype(q_out.dtype)

    ckv = p_ref[:, q_lora:].astype(F32)
    ckvn = (_rms(ckv, ckv.shape[-1]) * kva_ref[...]).astype(BF16)
    kn = _dot(ckvn, wkn_ref[...])
    v_out[...] = _dot(ckvn, wv_ref[...]).astype(v_out.dtype)
    kr = kr_ref[...]
    lane = lax.broadcasted_iota(jnp.int32, kr.shape, 1)
    kr = jnp.where(lane < MLA_ROPE, kr, 0.0)
    kr_ss = jnp.sum(kr * kr, axis=-1, keepdims=True)
    for h in range(N_HEADS):
        kb = kn[:, h * HEAD_DIM:(h + 1) * HEAD_DIM]
        inv = lax.rsqrt((jnp.sum(kb * kb, axis=-1, keepdims=True) + kr_ss) / MLA_QK + EPS)
        k_out[:, h * MLA_PAD:h * MLA_PAD + HEAD_DIM] = (kb * inv * gk[:, :HEAD_DIM]).astype(k_out.dtype)
        k_out[:, h * MLA_PAD + HEAD_DIM:(h + 1) * MLA_PAD] = _rope_lanes(
            kr * inv * gk[:, HEAD_DIM:], cos_t, sin_t).astype(k_out.dtype)


def mla_prep(proj, kr, qa, kva, wq, wkn, wv, gq, gk, cos_t, sin_t, seq, q_lora, kv_lora, tm=512):
    m = proj.shape[0]
    lat = q_lora + kv_lora
    lat_blk = (3 * N_HEADS * HEAD_DIM) // lat
    assert lat_blk * lat == 3 * N_HEADS * HEAD_DIM
    ns = seq // tm
    full = lambda i: (0, 0)
    kern = functools.partial(_mla_prep_kernel, q_lora=q_lora)
    return pl.pallas_call(
        kern,
        grid=(m // tm,),
        in_specs=[pl.BlockSpec((tm, lat), lambda i: (i, lat_blk)),
                  pl.BlockSpec((tm, HEAD_DIM), lambda i: (i, 0)),
                  pl.BlockSpec((1, q_lora), full),
                  pl.BlockSpec((1, kv_lora), full),
                  pl.BlockSpec(wq.shape, full),
                  pl.BlockSpec(wkn.shape, full),
                  pl.BlockSpec(wv.shape, full),
                  pl.BlockSpec((1, MLA_PAD), full),
                  pl.BlockSpec((1, MLA_PAD), full),
                  pl.BlockSpec((tm, HEAD_DIM), lambda i: (i % ns, 0)),
                  pl.BlockSpec((tm, HEAD_DIM), lambda i: (i % ns, 0))],
        out_specs=[pl.BlockSpec((tm, N_HEADS * MLA_PAD), lambda i: (i, 0)),
                   pl.BlockSpec((tm, N_HEADS * MLA_PAD), lambda i: (i, 0)),
                   pl.BlockSpec((tm, N_HEADS * HEAD_DIM), lambda i: (i, 0))],
        out_shape=[jax.ShapeDtypeStruct((m, N_HEADS * MLA_PAD), BF16),
                   jax.ShapeDtypeStruct((m, N_HEADS * MLA_PAD), BF16),
                   jax.ShapeDtypeStruct((m, N_HEADS * HEAD_DIM), BF16)],
        compiler_params=_params("parallel"),
        name="mla_prep",
    )(proj, kr, qa, kva, wq, wkn, wv, gq, gk, cos_t, sin_t)


def _softmax_block(q, ks, vs, state, scale, causal):
    m, l, acc = state
    z = _dot_nt(q, ks)
    if causal is not None:
        z = jnp.where(causal, z, MASK_NEG)
    m_new = jnp.maximum(m, jnp.max(z, axis=1, keepdims=True))
    p = jnp.exp((z - m_new) * scale)
    alpha = jnp.exp((m - m_new) * scale)
    l = alpha * l + jnp.sum(p, axis=1, keepdims=True)
    acc = alpha * acc + _dot(p.astype(BF16), vs)
    return m_new, l, acc


def _mla_attn_kernel(q_ref, k_ref, v_ref, o_ref, *, blk, scale):
    qi = pl.program_id(2)
    qa = q_ref[:blk, :]
    qb = q_ref[blk:, :]
    row = lax.broadcasted_iota(jnp.int32, (blk, blk), 0)
    col = lax.broadcasted_iota(jnp.int32, (blk, blk), 1)
    causal = col <= row
    block = functools.partial(_softmax_block, scale=scale)
    init = (jnp.full((blk, 1), MASK_NEG, F32), jnp.zeros((blk, 1), F32), jnp.zeros((blk, HEAD_DIM), F32))

    def body(t, c):
        sa, sb = c
        for u in range(2):
            ks, vs = _kv_block(k_ref, v_ref, 2 * t + u, blk)
            sa = block(qa, ks, vs, sa, causal=None)
            sb = block(qb, ks, vs, sb, causal=None)
        return sa, sb

    sa, sb = lax.fori_loop(0, qi, body, (init, init))
    ks, vs = _kv_block(k_ref, v_ref, 2 * qi, blk)
    sa = block(qa, ks, vs, sa, causal=causal)
    sb = block(qb, ks, vs, sb, causal=None)
    ks, vs = _kv_block(k_ref, v_ref, 2 * qi + 1, blk)
    sb = block(qb, ks, vs, sb, causal=causal)
    o_ref[:blk, :] = (sa[2] / sa[1]).astype(o_ref.dtype)
    o_ref[blk:, :] = (sb[2] / sb[1]).astype(o_ref.dtype)


def mla_attention(q, k, v, batch, seq, blk=256):
    nq = seq // (2 * blk)
    kern = functools.partial(_mla_attn_kernel, blk=blk, scale=MLA_QK ** -0.5)
    return pl.pallas_call(
        kern,
        grid=(batch, N_HEADS, nq),
        in_specs=[pl.BlockSpec((2 * blk, MLA_PAD), lambda b, h, i: (b * nq + i, h)),
                  pl.BlockSpec((seq, MLA_PAD), lambda b, h, i: (b, h)),
                  pl.BlockSpec((seq, HEAD_DIM), lambda b, h, i: (b, h))],
        out_specs=pl.BlockSpec((2 * blk, HEAD_DIM), lambda b, h, i: (b * nq + i, h)),
        out_shape=jax.ShapeDtypeStruct((batch * seq, N_HEADS * HEAD_DIM), BF16),
        compiler_params=_params("parallel", "parallel", "arbitrary"),
        name="mla_attention",
    )(q, k, v)


def _seg_cumsum(x, seg, row):
    pos = row & (seg - 1)
    s = 1
    while s < seg:
        x = x + jnp.where(pos >= s, pltpu.roll(x, s, 0), 0.0)
        s *= 2
    return x


def _seg_rev_cumsum_excl(x, seg, row):
    pos = row & (seg - 1)
    n = x.shape[0]
    y = x
    s = 1
    while s < seg:
        y = y + jnp.where(pos < seg - s, pltpu.roll(y, n - s, 0), 0.0)
        s *= 2
    return y - x


def _rebase_halves(x, half, second):
    pieces = []
    for s in range(0, x.shape[0], 2 * half):
        pivot = x[s + half - 1:s + half]
        lo, hi = x[s:s + half], x[s + half:s + 2 * half]
        pieces += [lo, hi - pivot] if second else [lo - pivot, hi]
    return jnp.concatenate(pieces, axis=0)


def _hgrn2_kernel(lb_ref, q_ref, z_ref, i_ref, g_ref, gn_ref, o_ref, st_ref, *, rows, layer):
    @pl.when(pl.program_id(2) == 0)
    def _():
        st_ref[...] = jnp.zeros_like(st_ref)

    lbp = lb_ref[...]
    e = jnp.exp(lbp - jnp.max(lbp, axis=0, keepdims=True))
    p = e / jnp.sum(e, axis=0, keepdims=True)
    lb = jnp.sum(p[:layer + 1], axis=0, keepdims=True) - p[0:1]

    z = z_ref[...]
    q = q_ref[...]
    v = i_ref[...]
    log_sig = jnp.minimum(z, 0.0) - jnp.log(1.0 + jnp.exp(-jnp.abs(z)))
    a = jnp.log(jnp.maximum(lb, LB_FLOOR))
    c = jnp.log1p(-lb) + log_sig
    log_f = jnp.maximum(a, c) + jnp.log(1.0 + jnp.exp(-jnp.abs(a - c)))
    k = 1.0 - jnp.exp(log_f)

    row = lax.broadcasted_iota(jnp.int32, (rows, HEAD_DIM), 0)
    fwd = {HG_CHUNK: _seg_cumsum(log_f, HG_CHUNK, row)}
    rev = {HG_CHUNK: _seg_rev_cumsum_excl(log_f, HG_CHUNK, row)}
    half = HG_CHUNK // 2
    while half >= 1:
        if half >= SUBLANES:
            fwd[half] = _rebase_halves(fwd[2 * half], half, second=True)
            rev[half] = _rebase_halves(rev[2 * half], half, second=False)
        else:
            fwd[half] = _seg_cumsum(log_f, half, row)
            rev[half] = _seg_rev_cumsum_excl(log_f, half, row)
        half //= 2
    b = fwd[HG_CHUNK]
    q_in = (q * jnp.exp(b)).astype(BF16)
    k_out = (k * jnp.exp(rev[HG_CHUNK])).astype(BF16)
    v16 = v.astype(BF16)
    diag = jnp.sum(q * k, axis=1, keepdims=True)

    levels = []
    half = 1
    while half < HG_CHUNK:
        second = (row & (2 * half - 1)) >= half
        ql = jnp.where(second, q * jnp.exp(fwd[half]), 0.0).astype(BF16)
        kl = jnp.where(second, 0.0, k * jnp.exp(rev[half])).astype(BF16)
        levels.append((2 * half, ql, kl))
        half *= 2

    ct = lax.broadcasted_iota(jnp.int32, (HG_CHUNK, HG_CHUNK), 0)
    cs = lax.broadcasted_iota(jnp.int32, (HG_CHUNK, HG_CHUNK), 1)
    st = st_ref[...]
    outs = []
    for ci in range(rows // HG_CHUNK):
        sl = slice(ci * HG_CHUNK, (ci + 1) * HG_CHUNK)
        attn = jnp.zeros((HG_CHUNK, HG_CHUNK), F32)
        for w, ql, kl in levels:
            part = _dot_nt(ql[sl], kl[sl])
            if w < HG_CHUNK:
                part = jnp.where((ct & -w) == (cs & -w), part, 0.0)
            attn = attn + part
        o = _dot(attn.astype(BF16), v16[sl]) + diag[sl] * v[sl] + _dot_nt(q_in[sl], st.astype(BF16))
        outs.append(o)
        b_last = b[(ci + 1) * HG_CHUNK - 1:(ci + 1) * HG_CHUNK, :]
        st = st * jnp.exp(b_last) + _dot_tn(v16[sl], k_out[sl])
    st_ref[...] = st

    o = jnp.concatenate(outs, axis=0)
    g = g_ref[...]
    o_ref[...] = (_rms(o, HEAD_DIM) * gn_ref[...] * (g * _sigmoid(g))).astype(o_ref.dtype)


def hgrn2(proj, lower_bounds, out_norm, layer, batch, seq, rows=256):
    depth = lower_bounds.shape[0]
    nr = seq // rows
    kern = functools.partial(_hgrn2_kernel, rows=rows, layer=layer)

    def col(c):
        return pl.BlockSpec((rows, HEAD_DIM), lambda b, h, r: (b * nr + r, c * N_HEADS + h))

    return pl.pallas_call(
        kern,
        grid=(batch, N_HEADS, nr),
        in_specs=[pl.BlockSpec((depth, HEAD_DIM), lambda b, h, r: (0, h)),
                  col(0), col(1), col(2), col(3),
                  pl.BlockSpec((1, HEAD_DIM), lambda b, h, r: (0, 0))],
        out_specs=pl.BlockSpec((rows, HEAD_DIM), lambda b, h, r: (b * nr + r, h)),
        out_shape=jax.ShapeDtypeStruct((batch * seq, N_HEADS * HEAD_DIM), BF16),
        scratch_shapes=[pltpu.VMEM((HEAD_DIM, HEAD_DIM), F32)],
        compiler_params=_params("parallel", "parallel", "arbitrary"),
        name="hgrn2",
    )(lower_bounds, proj, proj, proj, proj, out_norm.reshape(1, HEAD_DIM))


def _merge_kernel(a0, a1, a2, w0, w1, w2, g0, g1, g2, o_ref, wbf_ref):
    @pl.when(pl.program_id(1) == 0)
    def _():
        for n, w_ref in enumerate((w0, w1, w2)):
            _stage_weight(wbf_ref.at[n], w_ref)

    acc = _sigmoid(g0[...].astype(F32)) * _dot(a0[...], wbf_ref[0])
    acc = acc + _sigmoid(g1[...].astype(F32)) * _dot(a1[...], wbf_ref[1])
    acc = acc + _sigmoid(g2[...].astype(F32)) * _dot(a2[...], wbf_ref[2])
    o_ref[...] = acc.astype(o_ref.dtype)


def gated_merge(o_sb, o_mla, o_hg, w_sb, w_mla, w_hg, layer, gate_logits, tm=1024, tn=512):
    m, k = o_sb.shape
    n = w_sb.shape[2]
    nb = n // tn
    a_spec = pl.BlockSpec((tm, k), lambda j, i: (i, 0))
    w_spec = pl.BlockSpec((None, k, tn), lambda j, i: (layer, 0, j))

    def g_spec(branch):
        return pl.BlockSpec((tm, tn), lambda j, i: (i, branch * nb + j))

    return pl.pallas_call(
        _merge_kernel,
        grid=(nb, m // tm),
        in_specs=[a_spec, a_spec, a_spec, w_spec, w_spec, w_spec, g_spec(0), g_spec(1), g_spec(2)],
        out_specs=pl.BlockSpec((tm, tn), lambda j, i: (i, j)),
        out_shape=jax.ShapeDtypeStruct((m, n), BF16),
        scratch_shapes=[pltpu.VMEM((3, k, tn), BF16)],
        compiler_params=_params("arbitrary", "arbitrary"),
        name="gated_merge",
    )(o_sb, o_mla, o_hg, w_sb, w_mla, w_hg, gate_logits, gate_logits, gate_logits)


HALO = 16


def _up_conv_glu_kernel(a_ref, ah_ref, wg_ref, wv_ref, cg_ref, cv_ref, o_ref, wbf_ref, *, tiles_per_seq):
    @pl.when(pl.program_id(1) == 0)
    def _():
        _stage_weight(wbf_ref.at[0], wg_ref)
        _stage_weight(wbf_ref.at[1], wv_ref)

    a = a_ref[...]
    ah = ah_ref[...]
    seq_start = (pl.program_id(1) % tiles_per_seq) == 0
    row = lax.broadcasted_iota(jnp.int32, o_ref.shape, 0)

    def conv(slot, c_ref):
        w = wbf_ref[slot]
        u = _dot(a, w)
        halo = jnp.where(seq_start, 0.0, _dot(ah, w))
        prev1 = halo[HALO - 1:HALO]
        prev2 = halo[HALO - 2:HALO - 1]
        u1 = jnp.where(row == 0, prev1, pltpu.roll(u, 1, 0))
        u2 = jnp.where(row == 0, prev2, jnp.where(row == 1, prev1, pltpu.roll(u, 2, 0)))
        c = c_ref[...]
        return u2 * c[0:1] + u1 * c[1:2] + u * c[2:3]

    gate = conv(0, cg_ref)
    val = conv(1, cv_ref)
    o_ref[...] = (gate * _sigmoid(gate) * val).astype(o_ref.dtype)


def up_conv_glu(h, w_up, conv_w, layer, seq, tm=1024, tn=256):
    m, k = h.shape
    d_ff = w_up.shape[2] // 2
    nb = d_ff // tn
    kern = functools.partial(_up_conv_glu_kernel, tiles_per_seq=seq // tm)
    halo_blocks = tm // HALO
    return pl.pallas_call(
        kern,
        grid=(nb, m // tm),
        in_specs=[pl.BlockSpec((tm, k), lambda j, i: (i, 0)),
                  pl.BlockSpec((HALO, k), lambda j, i: (jnp.maximum(i * halo_blocks - 1, 0), 0)),
                  pl.BlockSpec((None, k, tn), lambda j, i: (layer, 0, j)),
                  pl.BlockSpec((None, k, tn), lambda j, i: (layer, 0, nb + j)),
                  pl.BlockSpec((None, conv_w.shape[1], tn), lambda j, i: (layer, 0, j)),
                  pl.BlockSpec((None, conv_w.shape[1], tn), lambda j, i: (layer, 0, nb + j))],
        out_specs=pl.BlockSpec((tm, tn), lambda j, i: (i, j)),
        out_shape=jax.ShapeDtypeStruct((m, d_ff), BF16),
        scratch_shapes=[pltpu.VMEM((2, k, tn), BF16)],
        compiler_params=_params("arbitrary", "arbitrary"),
        name="up_conv_glu",
    )(h, h, w_up, w_up, conv_w, conv_w)


def _rope_lane_tables(seq):
    pos = jnp.arange(seq, dtype=F32)
    inv = ROPE_THETA ** (-jnp.arange(0, MLA_ROPE, 2, dtype=F32) / MLA_ROPE)
    ang = pos[:, None] * inv[None, :]
    cos, sin = jnp.cos(ang), jnp.sin(ang)
    zeros = jnp.zeros((seq, HEAD_DIM - MLA_ROPE), F32)
    return (jnp.concatenate([cos, cos, zeros], axis=1),
            jnp.concatenate([-sin, sin, zeros], axis=1))


def _pad_head_gain(g):
    return jnp.pad(g, (0, MLA_PAD - MLA_QK)).reshape(1, MLA_PAD)


def kernel(x, lower_bounds, attn_norm, w_in, sb_q_norm, sb_k_norm, mla_q_a_norm, mla_kv_a_norm, mla_w_q_b, mla_w_kv_b, mla_q_norm, mla_k_norm, hg_out_norm, w_branch_sb, w_branch_mla, w_branch_hg, w_out, ffn_norm, w_up, ffn_conv, w_down):
    batch, seq, d_model = x.shape
    depth = w_in.shape[0]
    width = N_HEADS * HEAD_DIM
    q_lora = mla_w_q_b.shape[1]
    kv_lora = mla_w_kv_b.shape[1]
    c_attn = 3 * width + q_lora + kv_lora
    c_hg = c_attn + MLA_ROPE
    c_gate = c_hg + 4 * width

    cos_t, sin_t = _rope_lane_tables(seq)
    xf = x.reshape(batch * seq, d_model)
    w_in_t = jnp.swapaxes(w_in, 1, 2)
    for l in range(depth):
        wq = jnp.pad(mla_w_q_b[l], ((0, 0), (0, 0), (0, MLA_PAD - MLA_QK))).reshape(q_lora, N_HEADS * MLA_PAD)
        wkn = mla_w_kv_b[l][:, :, :HEAD_DIM].reshape(kv_lora, width)
        wv = mla_w_kv_b[l][:, :, HEAD_DIM:].reshape(kv_lora, width)

        h = rmsnorm_rows(xf, attn_norm[l])
        p_attn = weight_matmul(h, w_in_t, l, BF16, 1024, 512, transposed=True, n=c_attn)
        p_kr = weight_matmul(h, w_in_t, l, F32, 1024, HEAD_DIM, transposed=True, col0=c_attn, n=HEAD_DIM)
        p_hg = weight_matmul(h, w_in_t, l, F32, 1024, 512, transposed=True, col0=c_hg, n=c_gate - c_hg)
        p_gate = weight_matmul(h, w_in_t, l, BF16, 1024, 512, transposed=True, col0=c_gate)

        sb_qk = headnorm_qk(p_attn, jnp.stack([sb_q_norm[l], sb_k_norm[l]]).reshape(2, 1, HEAD_DIM), width)
        o_sb = sb_attention(sb_qk, p_attn, batch, seq)

        mq, mk, mv = mla_prep(p_attn, p_kr, mla_q_a_norm[l].reshape(1, q_lora),
                              mla_kv_a_norm[l].reshape(1, kv_lora), wq.astype(BF16), wkn.astype(BF16),
                              wv.astype(BF16), _pad_head_gain(mla_q_norm[l]), _pad_head_gain(mla_k_norm[l]),
                              cos_t, sin_t, seq, q_lora, kv_lora)
        o_mla = mla_attention(mq, mk, mv, batch, seq)

        o_hg = hgrn2(p_hg, lower_bounds, hg_out_norm[l], l, batch, seq)

        merged = gated_merge(o_sb, o_mla, o_hg, w_branch_sb, w_branch_mla, w_branch_hg, l, p_gate)
        xf = weight_matmul(merged, w_out, l, F32, 1024, 512, res=xf)

        h2 = rmsnorm_rows(xf, ffn_norm[l])
        act = up_conv_glu(h2, w_up, ffn_conv, l, seq)
        for part in range(2):
            xf = weight_matmul(act, w_down, l, F32, 1024, 512, k_part=(part, 2), res=xf)
    return xf.reshape(batch, seq, d_model)
```

```python
import functools

import jax
import jax.numpy as jnp
from jax import lax
from jax.experimental import pallas as pl
from jax.experimental.pallas import tpu as pltpu

F32 = jnp.float32
BF16 = jnp.bfloat16

SUBLANES = 8
HEAD_DIM = 128
N_HEADS = 8
MLA_ROPE = 64
MLA_QK = HEAD_DIM + MLA_ROPE
MLA_PAD = 256
HG_CHUNK = 64
ROPE_THETA = 10000.0
EPS = 1e-6
MASK_NEG = -1e30
LB_FLOOR = 1e-30

VMEM_LIMIT_BYTES = 56 * 1024 * 1024


def _params(*sem):
    return pltpu.CompilerParams(dimension_semantics=sem, vmem_limit_bytes=VMEM_LIMIT_BYTES)


def _dot(a, b):
    return jnp.dot(a, b, preferred_element_type=F32)


def _dot_nt(a, b):
    return lax.dot_general(a, b, (((1,), (1,)), ((), ())), preferred_element_type=F32)


def _dot_tn(a, b):
    return lax.dot_general(a, b, (((0,), (0,)), ((), ())), preferred_element_type=F32)


def _rms(x, width):
    return x * lax.rsqrt(jnp.sum(x * x, axis=-1, keepdims=True) / width + EPS)


def _sigmoid(x):
    return 0.5 * jnp.tanh(0.5 * x) + 0.5


def _rmsnorm_kernel(x_ref, g_ref, o_ref):
    x = x_ref[...]
    o_ref[...] = (_rms(x, x.shape[-1]) * g_ref[...]).astype(o_ref.dtype)


def rmsnorm_rows(x, g, tm=256):
    m, d = x.shape
    return pl.pallas_call(
        _rmsnorm_kernel,
        grid=(m // tm,),
        in_specs=[pl.BlockSpec((tm, d), lambda i: (i, 0)),
                  pl.BlockSpec((1, d), lambda i: (0, 0))],
        out_specs=pl.BlockSpec((tm, d), lambda i: (i, 0)),
        out_shape=jax.ShapeDtypeStruct((m, d), BF16),
        compiler_params=_params("parallel"),
        name="rmsnorm_rows",
    )(x, g.reshape(1, d))


CAST_ELEMS = 512 * 512


def _stage_weight(dst_ref, w_ref):
    nrows, ncols = dst_ref.shape
    rows = min(nrows, CAST_ELEMS // ncols)

    def body(r, carry):
        sl = pl.ds(pl.multiple_of(r * rows, rows), rows)
        dst_ref[sl, :] = w_ref[sl, :].astype(BF16)
        return carry

    lax.fori_loop(0, nrows // rows, body, 0)


def _wmm_kernel(*refs, transposed, residual):
    a_ref, w_ref = refs[0], refs[1]
    r_ref = refs[2] if residual else None
    o_ref, wbf_ref = refs[-2], refs[-1]

    @pl.when(pl.program_id(1) == 0)
    def _():
        _stage_weight(wbf_ref, w_ref.at[0] if transposed else w_ref)

    d = (_dot_nt if transposed else _dot)(a_ref[...], wbf_ref[...])
    if residual:
        d = r_ref[...] + d
    o_ref[...] = d.astype(o_ref.dtype)


def weight_matmul(a, w, layer, out_dtype, tm, tn, *, transposed=False, col0=0, n=None, k_part=(0, 1), res=None):
    m = a.shape[0]
    kp, parts = k_part
    k = a.shape[1] // parts
    if transposed:
        assert parts == 1 and w.shape[2] == k
        n = w.shape[1] - col0 if n is None else n
        w_spec = pl.BlockSpec((pl.Element(1), pl.Element(tn), pl.Element(k)),
                              lambda j, i: (layer, pl.multiple_of(col0 + j * tn, SUBLANES), 0))
        staged = (tn, k)
    else:
        assert col0 % tn == 0 and w.shape[1] == k * parts
        n = w.shape[2] - col0 if n is None else n
        w_spec = pl.BlockSpec((None, k, tn), lambda j, i: (layer, kp, col0 // tn + j))
        staged = (k, tn)
    in_specs = [pl.BlockSpec((tm, k), lambda j, i: (i, kp)), w_spec]
    args = [a, w]
    if res is not None:
        in_specs.append(pl.BlockSpec((tm, tn), lambda j, i: (i, j)))
        args.append(res)
    return pl.pallas_call(
        functools.partial(_wmm_kernel, transposed=transposed, residual=res is not None),
        grid=(n // tn, m // tm),
        in_specs=in_specs,
        out_specs=pl.BlockSpec((tm, tn), lambda j, i: (i, j)),
        out_shape=jax.ShapeDtypeStruct((m, n), out_dtype),
        scratch_shapes=[pltpu.VMEM(staged, BF16)],
        compiler_params=_params("arbitrary", "arbitrary"),
        name="weight_matmul",
    )(*args)


def _headnorm_kernel(x_ref, g_ref, o_ref):
    g = g_ref[0]
    for h in range(x_ref.shape[1] // HEAD_DIM):
        sl = slice(h * HEAD_DIM, (h + 1) * HEAD_DIM)
        x = x_ref[:, sl].astype(F32)
        o_ref[:, sl] = (_rms(x, HEAD_DIM) * g).astype(o_ref.dtype)


def headnorm_qk(proj, gains, width, tm=512):
    m = proj.shape[0]
    return pl.pallas_call(
        _headnorm_kernel,
        grid=(m // tm, 2),
        in_specs=[pl.BlockSpec((tm, width), lambda i, j: (i, j)),
                  pl.BlockSpec((1, 1, HEAD_DIM), lambda i, j: (j, 0, 0))],
        out_specs=pl.BlockSpec((tm, width), lambda i, j: (i, j)),
        out_shape=jax.ShapeDtypeStruct((m, 2 * width), BF16),
        compiler_params=_params("parallel", "arbitrary"),
        name="headnorm_qk",
    )(proj, gains)


NEVER = -(1 << 30)
SB_STAGES = 5
SB_TILE, SB_BLOCK, SB_THRESHOLD, SB_LAST = 0, 1, 2, 4


def _sb_walk_table(n_tiles, blk):
    idle = (0, 0, NEVER, NEVER, 0, 1.0)
    cols = [idle] * (SB_STAGES - 1)
    for tile in range(n_tiles):
        last = 2 * tile + 1
        for step in range(last + 1):
            cols.append((tile, last - step, (step - 1) * blk, step * blk, int(step == last), float(step > 0)))
    cols += [idle] * (SB_STAGES - 1)
    ints = jnp.array([[col[r] for col in cols] for r in range(5)], jnp.int32)
    keep = jnp.array([col[5] for col in cols], F32)
    return ints, keep


def _sb_attn_kernel(walk_ref, keep_ref, q_ref, k_ref, v_ref, o_ref, z1, hilo, z2, zi, wb, carry, acc, *, blk, scale):
    row = lax.broadcasted_iota(jnp.int32, (blk, blk), 0)
    col = lax.broadcasted_iota(jnp.int32, (blk, blk), 1)
    delta = col - row
    tri = (row >= col).astype(BF16)
    from_here = jnp.concatenate([tri, tri], axis=0)
    for ref in (z1, hilo, z2, zi, wb, carry, acc):
        ref[...] = jnp.zeros_like(ref)

    def rows(ref, block):
        return ref[pl.ds(pl.multiple_of(block * blk, blk), blk), :]

    def trip(t, _):
        p1, p2, p4, p5 = t + 4, t + 3, t + 1, t
        ks = rows(k_ref, walk_ref[SB_BLOCK, p1])
        vs = rows(v_ref, walk_ref[SB_BLOCK, p5])
        for c in range(2):
            acc[c] = acc[c] * keep_ref[p5] + _dot(wb[c], vs)
        for c in range(2):
            seen = delta < walk_ref[SB_THRESHOLD + c, p4]
            wb[c] = jnp.where(seen, jnp.exp(zi[c]), 0.0).astype(BF16)
        for c in range(2):
            zi[c] = z2[c] + _dot(hilo[c], from_here)
        for c in range(2):
            z = z1[c]
            seen = delta < walk_ref[SB_THRESHOLD + c, p2]
            log_1m_beta = -(jnp.maximum(z, 0.0) + jnp.log(1.0 + jnp.exp(-jnp.abs(z))))
            log_1m_beta = jnp.where(seen, log_1m_beta, 0.0)
            hi = log_1m_beta.astype(BF16)
            hilo[c, :, :blk] = hi
            hilo[c, :, blk:] = (log_1m_beta - hi.astype(F32)).astype(BF16)
            later = carry[c] * keep_ref[p2]
            z2[c] = z + jnp.concatenate([later] * (blk // HEAD_DIM), axis=1)
            carry[c] = later + jnp.sum(log_1m_beta, axis=1, keepdims=True)
        for c in range(2):
            z1[c] = _dot_nt(rows(q_ref, 2 * walk_ref[SB_TILE, p1] + c), ks) * scale

        @pl.when(walk_ref[SB_LAST, p5] == 1)
        def _():
            for c in range(2):
                o_ref[pl.ds(pl.multiple_of((2 * walk_ref[SB_TILE, p5] + c) * blk, blk), blk), :] = (
                    acc[c].astype(o_ref.dtype))

        return 0

    lax.fori_loop(0, keep_ref.shape[0] - (SB_STAGES - 1), trip, 0)


def sb_attention(qk, proj, batch, seq, blk=256):
    width = N_HEADS * HEAD_DIM
    walk, keep = _sb_walk_table(seq // (2 * blk), blk)
    kern = functools.partial(_sb_attn_kernel, blk=blk, scale=HEAD_DIM ** -0.5)
    grid_spec = pltpu.PrefetchScalarGridSpec(
        num_scalar_prefetch=2,
        grid=(batch, N_HEADS),
        in_specs=[pl.BlockSpec((seq, HEAD_DIM), lambda b, h, *_: (b, h)),
                  pl.BlockSpec((seq, HEAD_DIM), lambda b, h, *_: (b, N_HEADS + h)),
                  pl.BlockSpec((seq, HEAD_DIM), lambda b, h, *_: (b, 2 * N_HEADS + h))],
        out_specs=pl.BlockSpec((seq, HEAD_DIM), lambda b, h, *_: (b, h)),
        scratch_shapes=[pltpu.VMEM((2, blk, blk), F32),
                        pltpu.VMEM((2, blk, 2 * blk), BF16),
                        pltpu.VMEM((2, blk, blk), F32),
                        pltpu.VMEM((2, blk, blk), F32),
                        pltpu.VMEM((2, blk, blk), BF16),
                        pltpu.VMEM((2, blk, HEAD_DIM), F32),
                        pltpu.VMEM((2, blk, HEAD_DIM), F32)])
    return pl.pallas_call(
        kern,
        grid_spec=grid_spec,
        out_shape=jax.ShapeDtypeStruct((batch * seq, width), BF16),
        compiler_params=_params("parallel", "arbitrary"),
        name="sb_attention",
    )(walk, keep, qk, qk, proj)


def _rope_lanes(x, cos_t, sin_t):
    lane = lax.broadcasted_iota(jnp.int32, x.shape, 1)
    half = MLA_ROPE // 2
    partner = jnp.where(lane < half, pltpu.roll(x, HEAD_DIM - half, 1), pltpu.roll(x, half, 1))
    return x * cos_t + partner * sin_t


def _mla_prep_kernel(p_ref, kr_ref, qa_ref, kva_ref, wq_ref, wkn_ref, wv_ref, gq_ref, gk_ref,
                     cos_ref, sin_ref, q_out, k_out, v_out, *, q_lora):
    cos_t = cos_ref[...]
    sin_t = sin_ref[...]
    gq = gq_ref[...]
    gk = gk_ref[...]
    cq = p_ref[:, :q_lora].astype(F32)
    cqn = (_rms(cq, q_lora) * qa_ref[...]).astype(BF16)
    qraw = _dot(cqn, wq_ref[...])
    for h in range(N_HEADS):
        blk = qraw[:, h * MLA_PAD:(h + 1) * MLA_PAD]
        y = _rms(blk, MLA_QK) * gq
        q_out[:, h * MLA_PAD:h * MLA_PAD + HEAD_DIM] = y[:, :HEAD_DIM].astype(q_out.dtype)
        q_out[:, h * MLA_PAD + HEAD_DIM:(h + 1) * MLA_PAD] = _rope_lanes(
            y[:, HEAD_DIM:], cos_t, sin_t).astype(q_out.dtype)

    ckv = p_ref[:, q_lora:].astype(F32)
    ckvn = (_rms(ckv, ckv.shape[-1]) * kva_ref[...]).astype(BF16)
    kn = _dot(ckvn, wkn_ref[...])
    v_out[...] = _dot(ckvn, wv_ref[...]).astype(v_out.dtype)
    kr = kr_ref[...]
    lane = lax.broadcasted_iota(jnp.int32, kr.shape, 1)
    kr = jnp.where(lane < MLA_ROPE, kr, 0.0)
    kr_ss = jnp.sum(kr * kr, axis=-1, keepdims=True)
    for h in range(N_HEADS):
        kb = kn[:, h * HEAD_DIM:(h + 1) * HEAD_DIM]
        inv = lax.rsqrt((jnp.sum(kb * kb, axis=-1, keepdims=True) + kr_ss) / MLA_QK + EPS)
        k_out[:, h * MLA_PAD:h * MLA_PAD + HEAD_DIM] = (kb * inv * gk[:, :HEAD_DIM]).astype(k_out.dtype)
        k_out[:, h * MLA_PAD + HEAD_DIM:(h + 1) * MLA_PAD] = _rope_lanes(
            kr * inv * gk[:, HEAD_DIM:], cos_t, sin_t).astype(k_out.dtype)


def mla_prep(proj, kr, qa, kva, wq, wkn, wv, gq, gk, cos_t, sin_t, seq, q_lora, kv_lora, tm=512):
    m = proj.shape[0]
    lat = q_lora + kv_lora
    lat_blk = (3 * N_HEADS * HEAD_DIM) // lat
    assert lat_blk * lat == 3 * N_HEADS * HEAD_DIM
    ns = seq // tm
    full = lambda i: (0, 0)
    kern = functools.partial(_mla_prep_kernel, q_lora=q_lora)
    return pl.pallas_call(
        kern,
        grid=(m // tm,),
        in_specs=[pl.BlockSpec((tm, lat), lambda i: (i, lat_blk)),
                  pl.BlockSpec((tm, HEAD_DIM), lambda i: (i, 0)),
                  pl.BlockSpec((1, q_lora), full),
                  pl.BlockSpec((1, kv_lora), full),
                  pl.BlockSpec(wq.shape, full),
                  pl.BlockSpec(wkn.shape, full),
                  pl.BlockSpec(wv.shape, full),
                  pl.BlockSpec((1, MLA_PAD), full),
                  pl.BlockSpec((1, MLA_PAD), full),
                  pl.BlockSpec((tm, HEAD_DIM), lambda i: (i % ns, 0)),
                  pl.BlockSpec((tm, HEAD_DIM), lambda i: (i % ns, 0))],
        out_specs=[pl.BlockSpec((tm, N_HEADS * MLA_PAD), lambda i: (i, 0)),
                   pl.BlockSpec((tm, N_HEADS * MLA_PAD), lambda i: (i, 0)),
                   pl.BlockSpec((tm, N_HEADS * HEAD_DIM), lambda i: (i, 0))],
        out_shape=[jax.ShapeDtypeStruct((m, N_HEADS * MLA_PAD), BF16),
                   jax.ShapeDtypeStruct((m, N_HEADS * MLA_PAD), BF16),
                   jax.ShapeDtypeStruct((m, N_HEADS * HEAD_DIM), BF16)],
        compiler_params=_params("parallel"),
        name="mla_prep",
    )(proj, kr, qa, kva, wq, wkn, wv, gq, gk, cos_t, sin_t)


def _softmax_block(q, ks, vs, state, scale, causal):
    m, l, acc = state
    z = _dot_nt(q, ks)
    if causal is not None:
        z = jnp.where(causal, z, MASK_NEG)
    m_new = jnp.maximum(m, jnp.max(z, axis=1, keepdims=True))
    p = jnp.exp((z - m_new) * scale)
    alpha = jnp.exp((m - m_new) * scale)
    l = alpha * l + jnp.sum(p, axis=1, keepdims=True)
    acc = alpha * acc + _dot(p.astype(BF16), vs)
    return m_new, l, acc


def _mla_attn_kernel(q_ref, k_ref, v_ref, o_ref, *, blk, scale):
    qi = pl.program_id(2)
    qa = q_ref[:blk, :]
    qb = q_ref[blk:, :]
    row = lax.broadcasted_iota(jnp.int32, (blk, blk), 0)
    col = lax.broadcasted_iota(jnp.int32, (blk, blk), 1)
    causal = col <= row
    block = functools.partial(_softmax_block, scale=scale)
    init = (jnp.full((blk, 1), MASK_NEG, F32), jnp.zeros((blk, 1), F32), jnp.zeros((blk, HEAD_DIM), F32))

    def kv(j):
        start = pl.multiple_of(j * blk, blk)
        return k_ref[pl.ds(start, blk), :], v_ref[pl.ds(start, blk), :]

    def body(t, c):
        sa, sb = c
        for u in range(2):
            ks, vs = kv(2 * t + u)
            sa = block(qa, ks, vs, sa, causal=None)
            sb = block(qb, ks, vs, sb, causal=None)
        return sa, sb

    sa, sb = lax.fori_loop(0, qi, body, (init, init))
    ks, vs = kv(2 * qi)
    sa = block(qa, ks, vs, sa, causal=causal)
    sb = block(qb, ks, vs, sb, causal=None)
    ks, vs = kv(2 * qi + 1)
    sb = block(qb, ks, vs, sb, causal=causal)
    o_ref[:blk, :] = (sa[2] / sa[1]).astype(o_ref.dtype)
    o_ref[blk:, :] = (sb[2] / sb[1]).astype(o_ref.dtype)


def mla_attention(q, k, v, batch, seq, blk=256):
    nq = seq // (2 * blk)
    kern = functools.partial(_mla_attn_kernel, blk=blk, scale=MLA_QK ** -0.5)
    return pl.pallas_call(
        kern,
        grid=(batch, N_HEADS, nq),
        in_specs=[pl.BlockSpec((2 * blk, MLA_PAD), lambda b, h, i: (b * nq + i, h)),
                  pl.BlockSpec((seq, MLA_PAD), lambda b, h, i: (b, h)),
                  pl.BlockSpec((seq, HEAD_DIM), lambda b, h, i: (b, h))],
        out_specs=pl.BlockSpec((2 * blk, HEAD_DIM), lambda b, h, i: (b * nq + i, h)),
        out_shape=jax.ShapeDtypeStruct((batch * seq, N_HEADS * HEAD_DIM), BF16),
        compiler_params=_params("parallel", "parallel", "arbitrary"),
        name="mla_attention",
    )(q, k, v)


def _seg_cumsum(x, seg, row):
    pos = row & (seg - 1)
    s = 1
    while s < seg:
        x = x + jnp.where(pos >= s, pltpu.roll(x, s, 0), 0.0)
        s *= 2
    return x


def _seg_rev_cumsum_excl(x, seg, row):
    pos = row & (seg - 1)
    n = x.shape[0]
    y = x
    s = 1
    while s < seg:
        y = y + jnp.where(pos < seg - s, pltpu.roll(y, n - s, 0), 0.0)
        s *= 2
    return y - x


def _rebase_halves(x, half, second):
    pieces = []
    for s in range(0, x.shape[0], 2 * half):
        pivot = x[s + half - 1:s + half]
        lo, hi = x[s:s + half], x[s + half:s + 2 * half]
        pieces += [lo, hi - pivot] if second else [lo - pivot, hi]
    return jnp.concatenate(pieces, axis=0)


def _hgrn2_kernel(lb_ref, q_ref, z_ref, i_ref, g_ref, gn_ref, o_ref, st_ref, *, rows, layer):
    @pl.when(pl.program_id(2) == 0)
    def _():
        st_ref[...] = jnp.zeros_like(st_ref)

    lbp = lb_ref[...]
    e = jnp.exp(lbp - jnp.max(lbp, axis=0, keepdims=True))
    p = e / jnp.sum(e, axis=0, keepdims=True)
    lb = jnp.sum(p[:layer + 1], axis=0, keepdims=True) - p[0:1]

    z = z_ref[...]
    q = q_ref[...]
    v = i_ref[...]
    log_sig = jnp.minimum(z, 0.0) - jnp.log(1.0 + jnp.exp(-jnp.abs(z)))
    a = jnp.log(jnp.maximum(lb, LB_FLOOR))
    c = jnp.log1p(-lb) + log_sig
    log_f = jnp.maximum(a, c) + jnp.log(1.0 + jnp.exp(-jnp.abs(a - c)))
    k = 1.0 - jnp.exp(log_f)

    row = lax.broadcasted_iota(jnp.int32, (rows, HEAD_DIM), 0)
    fwd = {HG_CHUNK: _seg_cumsum(log_f, HG_CHUNK, row)}
    rev = {HG_CHUNK: _seg_rev_cumsum_excl(log_f, HG_CHUNK, row)}
    half = HG_CHUNK // 2
    while half >= 1:
        if half >= SUBLANES:
            fwd[half] = _rebase_halves(fwd[2 * half], half, second=True)
            rev[half] = _rebase_halves(rev[2 * half], half, second=False)
        else:
            fwd[half] = _seg_cumsum(log_f, half, row)
            rev[half] = _seg_rev_cumsum_excl(log_f, half, row)
        half //= 2
    b = fwd[HG_CHUNK]
    q_in = (q * jnp.exp(b)).astype(BF16)
    k_out = (k * jnp.exp(rev[HG_CHUNK])).astype(BF16)
    v16 = v.astype(BF16)
    diag = jnp.sum(q * k, axis=1, keepdims=True)

    levels = []
    half = 1
    while half < HG_CHUNK:
        second = (row & (2 * half - 1)) >= half
        ql = jnp.where(second, q * jnp.exp(fwd[half]), 0.0).astype(BF16)
        kl = jnp.where(second, 0.0, k * jnp.exp(rev[half])).astype(BF16)
        levels.append((2 * half, ql, kl))
        half *= 2

    ct = lax.broadcasted_iota(jnp.int32, (HG_CHUNK, HG_CHUNK), 0)
    cs = lax.broadcasted_iota(jnp.int32, (HG_CHUNK, HG_CHUNK), 1)
    st = st_ref[...]
    outs = []
    for ci in range(rows // HG_CHUNK):
        sl = slice(ci * HG_CHUNK, (ci + 1) * HG_CHUNK)
        attn = jnp.zeros((HG_CHUNK, HG_CHUNK), F32)
        for w, ql, kl in levels:
            part = _dot_nt(ql[sl], kl[sl])
            if w < HG_CHUNK:
                part = jnp.where((ct & -w) == (cs & -w), part, 0.0)
            attn = attn + part
        o = _dot(attn.astype(BF16), v16[sl]) + diag[sl] * v[sl] + _dot_nt(q_in[sl], st.astype(BF16))
        outs.append(o)
        b_last = b[(ci + 1) * HG_CHUNK - 1:(ci + 1) * HG_CHUNK, :]
        st = st * jnp.exp(b_last) + _dot_tn(v16[sl], k_out[sl])
    st_ref[...] = st

    o = jnp.concatenate(outs, axis=0)
    g = g_ref[...]
    o_ref[...] = (_rms(o, HEAD_DIM) * gn_ref[...] * (g * _sigmoid(g))).astype(o_ref.dtype)


def hgrn2(proj, lower_bounds, out_norm, layer, batch, seq, rows=256):
    depth = lower_bounds.shape[0]
    nr = seq // rows
    kern = functools.partial(_hgrn2_kernel, rows=rows, layer=layer)

    def col(c):
        return pl.BlockSpec((rows, HEAD_DIM), lambda b, h, r: (b * nr + r, c * N_HEADS + h))

    return pl.pallas_call(
        kern,
        grid=(batch, N_HEADS, nr),
        in_specs=[pl.BlockSpec((depth, HEAD_DIM), lambda b, h, r: (0, h)),
                  col(0), col(1), col(2), col(3),
                  pl.BlockSpec((1, HEAD_DIM), lambda b, h, r: (0, 0))],
        out_specs=pl.BlockSpec((rows, HEAD_DIM), lambda b, h, r: (b * nr + r, h)),
        out_shape=jax.ShapeDtypeStruct((batch * seq, N_HEADS * HEAD_DIM), BF16),
        scratch_shapes=[pltpu.VMEM((HEAD_DIM, HEAD_DIM), F32)],
        compiler_params=_params("parallel", "parallel", "arbitrary"),
        name="hgrn2",
    )(lower_bounds, proj, proj, proj, proj, out_norm.reshape(1, HEAD_DIM))


def _merge_kernel(a0, a1, a2, w0, w1, w2, g0, g1, g2, o_ref, wbf_ref):
    @pl.when(pl.program_id(1) == 0)
    def _():
        for n, w_ref in enumerate((w0, w1, w2)):
            _stage_weight(wbf_ref.at[n], w_ref)

    acc = _sigmoid(g0[...].astype(F32)) * _dot(a0[...], wbf_ref[0])
    acc = acc + _sigmoid(g1[...].astype(F32)) * _dot(a1[...], wbf_ref[1])
    acc = acc + _sigmoid(g2[...].astype(F32)) * _dot(a2[...], wbf_ref[2])
    o_ref[...] = acc.astype(o_ref.dtype)


def gated_merge(o_sb, o_mla, o_hg, w_sb, w_mla, w_hg, layer, gate_logits, tm=1024, tn=512):
    m, k = o_sb.shape
    n = w_sb.shape[2]
    nb = n // tn
    a_spec = pl.BlockSpec((tm, k), lambda j, i: (i, 0))
    w_spec = pl.BlockSpec((None, k, tn), lambda j, i: (layer, 0, j))

    def g_spec(branch):
        return pl.BlockSpec((tm, tn), lambda j, i: (i, branch * nb + j))

    return pl.pallas_call(
        _merge_kernel,
        grid=(nb, m // tm),
        in_specs=[a_spec, a_spec, a_spec, w_spec, w_spec, w_spec, g_spec(0), g_spec(1), g_spec(2)],
        out_specs=pl.BlockSpec((tm, tn), lambda j, i: (i, j)),
        out_shape=jax.ShapeDtypeStruct((m, n), BF16),
        scratch_shapes=[pltpu.VMEM((3, k, tn), BF16)],
        compiler_params=_params("arbitrary", "arbitrary"),
        name="gated_merge",
    )(o_sb, o_mla, o_hg, w_sb, w_mla, w_hg, gate_logits, gate_logits, gate_logits)


HALO = 16

def _up_conv_glu_kernel(a_ref, ah_ref, wg_ref, wv_ref, cg_ref, cv_ref, o_ref, wbf_ref, *, tiles_per_seq):
    @pl.when(pl.program_id(1) == 0)
    def _():
        _stage_weight(wbf_ref.at[0], wg_ref)
        _stage_weight(wbf_ref.at[1], wv_ref)

    a = a_ref[...]
    ah = ah_ref[...]
    seq_start = (pl.program_id(1) % tiles_per_seq) == 0
    row = lax.broadcasted_iota(jnp.int32, o_ref.shape, 0)

    def conv(slot, c_ref):
        w = wbf_ref[slot]
        u = _dot(a, w)
        halo = jnp.where(seq_start, 0.0, _dot(ah, w))
        prev1 = halo[HALO - 1:HALO]
        prev2 = halo[HALO - 2:HALO - 1]
        u1 = jnp.where(row == 0, prev1, pltpu.roll(u, 1, 0))
        u2 = jnp.where(row == 0, prev2, jnp.where(row == 1, prev1, pltpu.roll(u, 2, 0)))
        c = c_ref[...]
        return u2 * c[0:1] + u1 * c[1:2] + u * c[2:3]

    gate = conv(0, cg_ref)
    val = conv(1, cv_ref)
    o_ref[...] = (gate * _sigmoid(gate) * val).astype(o_ref.dtype)


def up_conv_glu(h, w_up, conv_w, layer, seq, tm=1024, tn=256):
    m, k = h.shape
    d_ff = w_up.shape[2] // 2
    nb = d_ff // tn
    kern = functools.partial(_up_conv_glu_kernel, tiles_per_seq=seq // tm)
    halo_blocks = tm // HALO
    return pl.pallas_call(
        kern,
        grid=(nb, m // tm),
        in_specs=[pl.BlockSpec((tm, k), lambda j, i: (i, 0)),
                  pl.BlockSpec((HALO, k), lambda j, i: (jnp.maximum(i * halo_blocks - 1, 0), 0)),
                  pl.BlockSpec((None, k, tn), lambda j, i: (layer, 0, j)),
                  pl.BlockSpec((None, k, tn), lambda j, i: (layer, 0, nb + j)),
                  pl.BlockSpec((None, conv_w.shape[1], tn), lambda j, i: (layer, 0, j)),
                  pl.BlockSpec((None, conv_w.shape[1], tn), lambda j, i: (layer, 0, nb + j))],
        out_specs=pl.BlockSpec((tm, tn), lambda j, i: (i, j)),
        out_shape=jax.ShapeDtypeStruct((m, d_ff), BF16),
        scratch_shapes=[pltpu.VMEM((2, k, tn), BF16)],
        compiler_params=_params("arbitrary", "arbitrary"),
        name="up_conv_glu",
    )(h, h, w_up, w_up, conv_w, conv_w)


def _rope_lane_tables(seq):
    pos = jnp.arange(seq, dtype=F32)
    inv = ROPE_THETA ** (-jnp.arange(0, MLA_ROPE, 2, dtype=F32) / MLA_ROPE)
    ang = pos[:, None] * inv[None, :]
    cos, sin = jnp.cos(ang), jnp.sin(ang)
    zeros = jnp.zeros((seq, HEAD_DIM - MLA_ROPE), F32)
    return (jnp.concatenate([cos, cos, zeros], axis=1),
            jnp.concatenate([-sin, sin, zeros], axis=1))


def _pad_head_gain(g):
    return jnp.pad(g, (0, MLA_PAD - MLA_QK)).reshape(1, MLA_PAD)


def kernel(x, lower_bounds, attn_norm, w_in, sb_q_norm, sb_k_norm, mla_q_a_norm, mla_kv_a_norm, mla_w_q_b, mla_w_kv_b, mla_q_norm, mla_k_norm, hg_out_norm, w_branch_sb, w_branch_mla, w_branch_hg, w_out, ffn_norm, w_up, ffn_conv, w_down):
    batch, seq, d_model = x.shape
    depth = w_in.shape[0]
    width = N_HEADS * HEAD_DIM
    q_lora = mla_w_q_b.shape[1]
    kv_lora = mla_w_kv_b.shape[1]
    c_attn = 3 * width + q_lora + kv_lora
    c_hg = c_attn + MLA_ROPE
    c_gate = c_hg + 4 * width

    cos_t, sin_t = _rope_lane_tables(seq)
    xf = x.reshape(batch * seq, d_model)
    w_in_t = jnp.swapaxes(w_in, 1, 2)
    for l in range(depth):
        wq = jnp.pad(mla_w_q_b[l], ((0, 0), (0, 0), (0, MLA_PAD - MLA_QK))).reshape(q_lora, N_HEADS * MLA_PAD)
        wkn = mla_w_kv_b[l][:, :, :HEAD_DIM].reshape(kv_lora, width)
        wv = mla_w_kv_b[l][:, :, HEAD_DIM:].reshape(kv_lora, width)

        h = rmsnorm_rows(xf, attn_norm[l])
        p_attn = weight_matmul(h, w_in_t, l, BF16, 1024, 512, transposed=True, n=c_attn)
        p_kr = weight_matmul(h, w_in_t, l, F32, 1024, HEAD_DIM, transposed=True, col0=c_attn, n=HEAD_DIM)
        p_hg = weight_matmul(h, w_in_t, l, F32, 1024, 512, transposed=True, col0=c_hg, n=c_gate - c_hg)
        p_gate = weight_matmul(h, w_in_t, l, BF16, 1024, 512, transposed=True, col0=c_gate)

        sb_qk = headnorm_qk(p_attn, jnp.stack([sb_q_norm[l], sb_k_norm[l]]).reshape(2, 1, HEAD_DIM), width)
        o_sb = sb_attention(sb_qk, p_attn, batch, seq)

        mq, mk, mv = mla_prep(p_attn, p_kr, mla_q_a_norm[l].reshape(1, q_lora),
                              mla_kv_a_norm[l].reshape(1, kv_lora), wq.astype(BF16), wkn.astype(BF16),
                              wv.astype(BF16), _pad_head_gain(mla_q_norm[l]), _pad_head_gain(mla_k_norm[l]),
                              cos_t, sin_t, seq, q_lora, kv_lora)
        o_mla = mla_attention(mq, mk, mv, batch, seq)

        o_hg = hgrn2(p_hg, lower_bounds, hg_out_norm[l], l, batch, seq)

        merged = gated_merge(o_sb, o_mla, o_hg, w_branch_sb, w_branch_mla, w_branch_hg, l, p_gate)
        xf = weight_matmul(merged, w_out, l, F32, 1024, 512, res=xf)

        h2 = rmsnorm_rows(xf, ffn_norm[l])
        act = up_conv_glu(h2, w_up, ffn_conv, l, seq)
        for part in range(2):
            xf = weight_matmul(act, w_down, l, F32, 1024, 512, k_part=(part, 2), res=xf)
    return xf.reshape(batch, seq, d_model)
```

```python
import functools

import jax
import jax.numpy as jnp
from jax import lax
from jax.experimental import pallas as pl
from jax.experimental.pallas import tpu as pltpu

F32 = jnp.float32
BF16 = jnp.bfloat16

SUBLANES = 8
HEAD_DIM = 128
N_HEADS = 8
MLA_ROPE = 64
MLA_QK = HEAD_DIM + MLA_ROPE
MLA_PAD = 256
HG_CHUNK = 64
ROPE_THETA = 10000.0
EPS = 1e-6
MASK_NEG = -1e30
LB_FLOOR = 1e-30

VMEM_LIMIT_BYTES = 56 * 1024 * 1024


def _params(*sem):
    return pltpu.CompilerParams(dimension_semantics=sem, vmem_limit_bytes=VMEM_LIMIT_BYTES)


def _dot(a, b):
    return jnp.dot(a, b, preferred_element_type=F32)


def _dot_nt(a, b):
    return lax.dot_general(a, b, (((1,), (1,)), ((), ())), preferred_element_type=F32)


def _dot_tn(a, b):
    return lax.dot_general(a, b, (((0,), (0,)), ((), ())), preferred_element_type=F32)


def _rms(x, width):
    return x * lax.rsqrt(jnp.sum(x * x, axis=-1, keepdims=True) / width + EPS)


def _sigmoid(x):
    return 0.5 * jnp.tanh(0.5 * x) + 0.5


def _rmsnorm_kernel(x_ref, g_ref, o_ref):
    x = x_ref[...]
    o_ref[...] = (_rms(x, x.shape[-1]) * g_ref[...]).astype(o_ref.dtype)


def rmsnorm_rows(x, g, tm=256):
    m, d = x.shape
    return pl.pallas_call(
        _rmsnorm_kernel,
        grid=(m // tm,),
        in_specs=[pl.BlockSpec((tm, d), lambda i: (i, 0)),
                  pl.BlockSpec((1, d), lambda i: (0, 0))],
        out_specs=pl.BlockSpec((tm, d), lambda i: (i, 0)),
        out_shape=jax.ShapeDtypeStruct((m, d), BF16),
        compiler_params=_params("parallel"),
        name="rmsnorm_rows",
    )(x, g.reshape(1, d))


CAST_ELEMS = 512 * 512


def _stage_weight(dst_ref, w_ref):
    nrows, ncols = dst_ref.shape
    rows = min(nrows, CAST_ELEMS // ncols)

    def body(r, carry):
        sl = pl.ds(pl.multiple_of(r * rows, rows), rows)
        dst_ref[sl, :] = w_ref[sl, :].astype(BF16)
        return carry

    lax.fori_loop(0, nrows // rows, body, 0)


def _wmm_kernel(*refs, transposed, residual):
    a_ref, w_ref = refs[0], refs[1]
    r_ref = refs[2] if residual else None
    o_ref, wbf_ref = refs[-2], refs[-1]

    @pl.when(pl.program_id(1) == 0)
    def _():
        _stage_weight(wbf_ref, w_ref.at[0] if transposed else w_ref)

    d = (_dot_nt if transposed else _dot)(a_ref[...], wbf_ref[...])
    if residual:
        d = r_ref[...] + d
    o_ref[...] = d.astype(o_ref.dtype)


def weight_matmul(a, w, layer, out_dtype, tm, tn, *, transposed=False, col0=0, n=None, k_part=(0, 1), res=None):
    m = a.shape[0]
    kp, parts = k_part
    k = a.shape[1] // parts
    if transposed:
        assert parts == 1 and w.shape[2] == k
        n = w.shape[1] - col0 if n is None else n
        w_spec = pl.BlockSpec((pl.Element(1), pl.Element(tn), pl.Element(k)),
                              lambda j, i: (layer, pl.multiple_of(col0 + j * tn, SUBLANES), 0))
        staged = (tn, k)
    else:
        assert col0 % tn == 0 and w.shape[1] == k * parts
        n = w.shape[2] - col0 if n is None else n
        w_spec = pl.BlockSpec((None, k, tn), lambda j, i: (layer, kp, col0 // tn + j))
        staged = (k, tn)
    in_specs = [pl.BlockSpec((tm, k), lambda j, i: (i, kp)), w_spec]
    args = [a, w]
    if res is not None:
        in_specs.append(pl.BlockSpec((tm, tn), lambda j, i: (i, j)))
        args.append(res)
    return pl.pallas_call(
        functools.partial(_wmm_kernel, transposed=transposed, residual=res is not None),
        grid=(n // tn, m // tm),
        in_specs=in_specs,
        out_specs=pl.BlockSpec((tm, tn), lambda j, i: (i, j)),
        out_shape=jax.ShapeDtypeStruct((m, n), out_dtype),
        scratch_shapes=[pltpu.VMEM(staged, BF16)],
        compiler_params=_params("arbitrary", "arbitrary"),
        name="weight_matmul",
    )(*args)


def _headnorm_kernel(x_ref, g_ref, o_ref):
    g = g_ref[0]
    for h in range(x_ref.shape[1] // HEAD_DIM):
        sl = slice(h * HEAD_DIM, (h + 1) * HEAD_DIM)
        x = x_ref[:, sl].astype(F32)
        o_ref[:, sl] = (_rms(x, HEAD_DIM) * g).astype(o_ref.dtype)


def headnorm_qk(proj, gains, width, tm=512):
    m = proj.shape[0]
    return pl.pallas_call(
        _headnorm_kernel,
        grid=(m // tm, 2),
        in_specs=[pl.BlockSpec((tm, width), lambda i, j: (i, j)),
                  pl.BlockSpec((1, 1, HEAD_DIM), lambda i, j: (j, 0, 0))],
        out_specs=pl.BlockSpec((tm, width), lambda i, j: (i, j)),
        out_shape=jax.ShapeDtypeStruct((m, 2 * width), BF16),
        compiler_params=_params("parallel", "arbitrary"),
        name="headnorm_qk",
    )(proj, gains)


NEVER = -(1 << 30)
SB_STAGES = 5
SB_TILE, SB_BLOCK, SB_THRESHOLD, SB_LAST = 0, 1, 2, 4


def _sb_walk_table(n_tiles, blk):
    idle = (0, 0, NEVER, NEVER, 0, 1.0)
    cols = [idle] * (SB_STAGES - 1)
    for tile in range(n_tiles):
        last = 2 * tile + 1
        for step in range(last + 1):
            cols.append((tile, last - step, (step - 1) * blk, step * blk, int(step == last), float(step > 0)))
    cols += [idle] * (SB_STAGES - 1)
    ints = jnp.array([[col[r] for col in cols] for r in range(5)], jnp.int32)
    keep = jnp.array([col[5] for col in cols], F32)
    return ints, keep


def _sb_attn_kernel(walk_ref, keep_ref, q_ref, k_ref, v_ref, o_ref, z1, hilo, z2, zi, wb, carry, acc, *, blk, scale):
    row = lax.broadcasted_iota(jnp.int32, (blk, blk), 0)
    col = lax.broadcasted_iota(jnp.int32, (blk, blk), 1)
    delta = col - row
    tri = (row >= col).astype(BF16)
    from_here = jnp.concatenate([tri, tri], axis=0)
    for ref in (z1, hilo, z2, zi, wb, carry, acc):
        ref[...] = jnp.zeros_like(ref)

    def rows(ref, block):
        return ref[pl.ds(pl.multiple_of(block * blk, blk), blk), :]

    def trip(t, _):
        p1, p2, p4, p5 = t + 4, t + 3, t + 1, t
        ks = rows(k_ref, walk_ref[SB_BLOCK, p1])
        vs = rows(v_ref, walk_ref[SB_BLOCK, p5])
        for c in range(2):
            acc[c] = acc[c] * keep_ref[p5] + _dot(wb[c], vs)
        for c in range(2):
            seen = delta < walk_ref[SB_THRESHOLD + c, p4]
            wb[c] = jnp.where(seen, jnp.exp(zi[c]), 0.0).astype(BF16)
        for c in range(2):
            zi[c] = z2[c] + _dot(hilo[c], from_here)
        for c in range(2):
            z = z1[c]
            seen = delta < walk_ref[SB_THRESHOLD + c, p2]
            log_1m_beta = -(jnp.maximum(z, 0.0) + jnp.log(1.0 + jnp.exp(-jnp.abs(z))))
            log_1m_beta = jnp.where(seen, log_1m_beta, 0.0)
            hi = log_1m_beta.astype(BF16)
            hilo[c, :, :blk] = hi
            hilo[c, :, blk:] = (log_1m_beta - hi.astype(F32)).astype(BF16)
            later = carry[c] * keep_ref[p2]
            z2[c] = z + jnp.concatenate([later] * (blk // HEAD_DIM), axis=1)
            carry[c] = later + jnp.sum(log_1m_beta, axis=1, keepdims=True)
        for c in range(2):
            z1[c] = _dot_nt(rows(q_ref, 2 * walk_ref[SB_TILE, p1] + c), ks) * scale

        @pl.when(walk_ref[SB_LAST, p5] == 1)
        def _():
            for c in range(2):
                o_ref[pl.ds(pl.multiple_of((2 * walk_ref[SB_TILE, p5] + c) * blk, blk), blk), :] = (
                    acc[c].astype(o_ref.dtype))

        return 0

    lax.fori_loop(0, keep_ref.shape[0] - (SB_STAGES - 1), trip, 0)


def sb_attention(qk, proj, batch, seq, blk=256):
    width = N_HEADS * HEAD_DIM
    walk, keep = _sb_walk_table(seq // (2 * blk), blk)
    kern = functools.partial(_sb_attn_kernel, blk=blk, scale=HEAD_DIM ** -0.5)
    grid_spec = pltpu.PrefetchScalarGridSpec(
        num_scalar_prefetch=2,
        grid=(batch, N_HEADS),
        in_specs=[pl.BlockSpec((seq, HEAD_DIM), lambda b, h, *_: (b, h)),
                  pl.BlockSpec((seq, HEAD_DIM), lambda b, h, *_: (b, N_HEADS + h)),
                  pl.BlockSpec((seq, HEAD_DIM), lambda b, h, *_: (b, 2 * N_HEADS + h))],
        out_specs=pl.BlockSpec((seq, HEAD_DIM), lambda b, h, *_: (b, h)),
        scratch_shapes=[pltpu.VMEM((2, blk, blk), F32),
                        pltpu.VMEM((2, blk, 2 * blk), BF16),
                        pltpu.VMEM((2, blk, blk), F32),
                        pltpu.VMEM((2, blk, blk), F32),
                        pltpu.VMEM((2, blk, blk), BF16),
                        pltpu.VMEM((2, blk, HEAD_DIM), F32),
                        pltpu.VMEM((2, blk, HEAD_DIM), F32)])
    return pl.pallas_call(
        kern,
        grid_spec=grid_spec,
        out_shape=jax.ShapeDtypeStruct((batch * seq, width), BF16),
        compiler_params=_params("parallel", "arbitrary"),
        name="sb_attention",
    )(walk, keep, qk, qk, proj)


def _rope_lanes(x, cos_t, sin_t):
    lane = lax.broadcasted_iota(jnp.int32, x.shape, 1)
    half = MLA_ROPE // 2
    partner = jnp.where(lane < half, pltpu.roll(x, HEAD_DIM - half, 1), pltpu.roll(x, half, 1))
    return x * cos_t + partner * sin_t


def _mla_prep_kernel(p_ref, kr_ref, qa_ref, kva_ref, wq_ref, wkn_ref, wv_ref, gq_ref, gk_ref,
                     cos_ref, sin_ref, q_out, k_out, v_out, *, q_lora):
    cos_t = cos_ref[...]
    sin_t = sin_ref[...]
    gq = gq_ref[...]
    gk = gk_ref[...]
    cq = p_ref[:, :q_lora].astype(F32)
    cqn = (_rms(cq, q_lora) * qa_ref[...]).astype(BF16)
    qraw = _dot(cqn, wq_ref[...])
    for h in range(N_HEADS):
        blk = qraw[:, h * MLA_PAD:(h + 1) * MLA_PAD]
        y = _rms(blk, MLA_QK) * gq
        q_out[:, h * MLA_PAD:h * MLA_PAD + HEAD_DIM] = y[:, :HEAD_DIM].astype(q_out.dtype)
        q_out[:, h * MLA_PAD + HEAD_DIM:(h + 1) * MLA_PAD] = _rope_lanes(
            y[:, HEAD_DIM:], cos_t, sin_t).astype(q_out.dtype)

    ckv = p_ref[:, q_lora:].astype(F32)
    ckvn = (_rms(ckv, ckv.shape[-1]) * kva_ref[...]).astype(BF16)
    kn = _dot(ckvn, wkn_ref[...])
    v_out[...] = _dot(ckvn, wv_ref[...]).astype(v_out.dtype)
    kr = kr_ref[...]
    lane = lax.broadcasted_iota(jnp.int32, kr.shape, 1)
    kr = jnp.where(lane < MLA_ROPE, kr, 0.0)
    kr_ss = jnp.sum(kr * kr, axis=-1, keepdims=True)
    for h in range(N_HEADS):
        kb = kn[:, h * HEAD_DIM:(h + 1) * HEAD_DIM]
        inv = lax.rsqrt((jnp.sum(kb * kb, axis=-1, keepdims=True) + kr_ss) / MLA_QK + EPS)
        k_out[:, h * MLA_PAD:h * MLA_PAD + HEAD_DIM] = (kb * inv * gk[:, :HEAD_DIM]).astype(k_out.dtype)
        k_out[:, h * MLA_PAD + HEAD_DIM:(h + 1) * MLA_PAD] = _rope_lanes(
            kr * inv * gk[:, HEAD_DIM:], cos_t, sin_t).astype(k_out.dtype)


def mla_prep(proj, kr, qa, kva, wq, wkn, wv, gq, gk, cos_t, sin_t, seq, q_lora, kv_lora, tm=512):
    m = proj.shape[0]
    lat = q_lora + kv_lora
    lat_blk = (3 * N_HEADS * HEAD_DIM) // lat
    assert lat_blk * lat == 3 * N_HEADS * HEAD_DIM
    ns = seq // tm
    full = lambda i: (0, 0)
    kern = functools.partial(_mla_prep_kernel, q_lora=q_lora)
    return pl.pallas_call(
        kern,
        grid=(m // tm,),
        in_specs=[pl.BlockSpec((tm, lat), lambda i: (i, lat_blk)),
                  pl.BlockSpec((tm, HEAD_DIM), lambda i: (i, 0)),
                  pl.BlockSpec((1, q_lora), full),
                  pl.BlockSpec((1, kv_lora), full),
                  pl.BlockSpec(wq.shape, full),
                  pl.BlockSpec(wkn.shape, full),
                  pl.BlockSpec(wv.shape, full),
                  pl.BlockSpec((1, MLA_PAD), full),
                  pl.BlockSpec((1, MLA_PAD), full),
                  pl.BlockSpec((tm, HEAD_DIM), lambda i: (i % ns, 0)),
                  pl.BlockSpec((tm, HEAD_DIM), lambda i: (i % ns, 0))],
        out_specs=[pl.BlockSpec((tm, N_HEADS * MLA_PAD), lambda i: (i, 0)),
                   pl.BlockSpec((tm, N_HEADS * MLA_PAD), lambda i: (i, 0)),
                   pl.BlockSpec((tm, N_HEADS * HEAD_DIM), lambda i: (i, 0))],
        out_shape=[jax.ShapeDtypeStruct((m, N_HEADS * MLA_PAD), BF16),
                   jax.ShapeDtypeStruct((m, N_HEADS * MLA_PAD), BF16),
                   jax.ShapeDtypeStruct((m, N_HEADS * HEAD_DIM), BF16)],
        compiler_params=_params("parallel"),
        name="mla_prep",
    )(proj, kr, qa, kva, wq, wkn, wv, gq, gk, cos_t, sin_t)


def _softmax_block(q, ks, vs, state, scale, causal):
    m, l, acc = state
    z = _dot_nt(q, ks)
    if causal is not None:
        z = jnp.where(causal, z, MASK_NEG)
    m_new = jnp.maximum(m, jnp.max(z, axis=1, keepdims=True))
    p = jnp.exp((z - m_new) * scale)
    alpha = jnp.exp((m - m_new) * scale)
    l = alpha * l + jnp.sum(p, axis=1, keepdims=True)
    acc = alpha * acc + _dot(p.astype(BF16), vs)
    return m_new, l, acc


def _mla_attn_kernel(q_ref, k_ref, v_ref, o_ref, *, blk, scale):
    qi = pl.program_id(2)
    qa = q_ref[:blk, :]
    qb = q_ref[blk:, :]
    row = lax.broadcasted_iota(jnp.int32, (blk, 2 * blk), 0)
    col = lax.broadcasted_iota(jnp.int32, (blk, 2 * blk), 1)
    block = functools.partial(_softmax_block, scale=scale)
    init = (jnp.full((blk, 1), MASK_NEG, F32), jnp.zeros((blk, 1), F32), jnp.zeros((blk, HEAD_DIM), F32))

    def kv(j):
        start = pl.multiple_of(j * 2 * blk, 2 * blk)
        return k_ref[pl.ds(start, 2 * blk), :], v_ref[pl.ds(start, 2 * blk), :]

    def body(t, c):
        sa, sb = c
        ks, vs = kv(t)
        return block(qa, ks, vs, sa, causal=None), block(qb, ks, vs, sb, causal=None)

    sa, sb = lax.fori_loop(0, qi, body, (init, init))
    ks, vs = kv(qi)
    sa = block(qa, ks[:blk], vs[:blk], sa, causal=(col <= row)[:, :blk])
    sb = block(qb, ks, vs, sb, causal=col <= row + blk)
    o_ref[:blk, :] = (sa[2] / sa[1]).astype(o_ref.dtype)
    o_ref[blk:, :] = (sb[2] / sb[1]).astype(o_ref.dtype)


def mla_attention(q, k, v, batch, seq, blk=256):
    nq = seq // (2 * blk)
    kern = functools.partial(_mla_attn_kernel, blk=blk, scale=MLA_QK ** -0.5)
    return pl.pallas_call(
        kern,
        grid=(batch, N_HEADS, nq),
        in_specs=[pl.BlockSpec((2 * blk, MLA_PAD), lambda b, h, i: (b * nq + i, h)),
                  pl.BlockSpec((seq, MLA_PAD), lambda b, h, i: (b, h)),
                  pl.BlockSpec((seq, HEAD_DIM), lambda b, h, i: (b, h))],
        out_specs=pl.BlockSpec((2 * blk, HEAD_DIM), lambda b, h, i: (b * nq + i, h)),
        out_shape=jax.ShapeDtypeStruct((batch * seq, N_HEADS * HEAD_DIM), BF16),
        compiler_params=_params("parallel", "parallel", "arbitrary"),
        name="mla_attention",
    )(q, k, v)


def _seg_cumsum(x, seg, row):
    pos = row & (seg - 1)
    s = 1
    while s < seg:
        x = x + jnp.where(pos >= s, pltpu.roll(x, s, 0), 0.0)
        s *= 2
    return x


def _seg_rev_cumsum_excl(x, seg, row):
    pos = row & (seg - 1)
    n = x.shape[0]
    y = x
    s = 1
    while s < seg:
        y = y + jnp.where(pos < seg - s, pltpu.roll(y, n - s, 0), 0.0)
        s *= 2
    return y - x


def _rebase_halves(x, half, second):
    pieces = []
    for s in range(0, x.shape[0], 2 * half):
        pivot = x[s + half - 1:s + half]
        lo, hi = x[s:s + half], x[s + half:s + 2 * half]
        pieces += [lo, hi - pivot] if second else [lo - pivot, hi]
    return jnp.concatenate(pieces, axis=0)


def _hgrn2_kernel(lb_ref, q_ref, z_ref, i_ref, g_ref, gn_ref, o_ref, st_ref, *, rows, layer):
    @pl.when(pl.program_id(2) == 0)
    def _():
        st_ref[...] = jnp.zeros_like(st_ref)

    lbp = lb_ref[...]
    e = jnp.exp(lbp - jnp.max(lbp, axis=0, keepdims=True))
    p = e / jnp.sum(e, axis=0, keepdims=True)
    lb = jnp.sum(p[:layer + 1], axis=0, keepdims=True) - p[0:1]

    z = z_ref[...]
    q = q_ref[...]
    v = i_ref[...]
    log_sig = jnp.minimum(z, 0.0) - jnp.log(1.0 + jnp.exp(-jnp.abs(z)))
    a = jnp.log(jnp.maximum(lb, LB_FLOOR))
    c = jnp.log1p(-lb) + log_sig
    log_f = jnp.maximum(a, c) + jnp.log(1.0 + jnp.exp(-jnp.abs(a - c)))
    k = 1.0 - jnp.exp(log_f)

    row = lax.broadcasted_iota(jnp.int32, (rows, HEAD_DIM), 0)
    fwd = {HG_CHUNK: _seg_cumsum(log_f, HG_CHUNK, row)}
    rev = {HG_CHUNK: _seg_rev_cumsum_excl(log_f, HG_CHUNK, row)}
    half = HG_CHUNK // 2
    while half >= 1:
        if half >= SUBLANES:
            fwd[half] = _rebase_halves(fwd[2 * half], half, second=True)
            rev[half] = _rebase_halves(rev[2 * half], half, second=False)
        else:
            fwd[half] = _seg_cumsum(log_f, half, row)
            rev[half] = _seg_rev_cumsum_excl(log_f, half, row)
        half //= 2
    b = fwd[HG_CHUNK]
    q_in = (q * jnp.exp(b)).astype(BF16)
    k_out = (k * jnp.exp(rev[HG_CHUNK])).astype(BF16)
    v16 = v.astype(BF16)
    diag = jnp.sum(q * k, axis=1, keepdims=True)

    levels = []
    half = 1
    while half < HG_CHUNK:
        second = (row & (2 * half - 1)) >= half
        ql = jnp.where(second, q * jnp.exp(fwd[half]), 0.0).astype(BF16)
        kl = jnp.where(second, 0.0, k * jnp.exp(rev[half])).astype(BF16)
        levels.append((2 * half, ql, kl))
        half *= 2

    ct = lax.broadcasted_iota(jnp.int32, (HG_CHUNK, HG_CHUNK), 0)
    cs = lax.broadcasted_iota(jnp.int32, (HG_CHUNK, HG_CHUNK), 1)
    st = st_ref[...]
    outs = []
    for ci in range(rows // HG_CHUNK):
        sl = slice(ci * HG_CHUNK, (ci + 1) * HG_CHUNK)
        attn = jnp.zeros((HG_CHUNK, HG_CHUNK), F32)
        for w, ql, kl in levels:
            part = _dot_nt(ql[sl], kl[sl])
            if w < HG_CHUNK:
                part = jnp.where((ct & -w) == (cs & -w), part, 0.0)
            attn = attn + part
        o = _dot(attn.astype(BF16), v16[sl]) + diag[sl] * v[sl] + _dot_nt(q_in[sl], st.astype(BF16))
        outs.append(o)
        b_last = b[(ci + 1) * HG_CHUNK - 1:(ci + 1) * HG_CHUNK, :]
        st = st * jnp.exp(b_last) + _dot_tn(v16[sl], k_out[sl])
    st_ref[...] = st

    o = jnp.concatenate(outs, axis=0)
    g = g_ref[...]
    o_ref[...] = (_rms(o, HEAD_DIM) * gn_ref[...] * (g * _sigmoid(g))).astype(o_ref.dtype)


def hgrn2(proj, lower_bounds, out_norm, layer, batch, seq, rows=1024):
    depth = lower_bounds.shape[0]
    nr = seq // rows
    kern = functools.partial(_hgrn2_kernel, rows=rows, layer=layer)

    def col(c):
        return pl.BlockSpec((rows, HEAD_DIM), lambda b, h, r: (b * nr + r, c * N_HEADS + h))

    return pl.pallas_call(
        kern,
        grid=(batch, N_HEADS, nr),
        in_specs=[pl.BlockSpec((depth, HEAD_DIM), lambda b, h, r: (0, h)),
                  col(0), col(1), col(2), col(3),
                  pl.BlockSpec((1, HEAD_DIM), lambda b, h, r: (0, 0))],
        out_specs=pl.BlockSpec((rows, HEAD_DIM), lambda b, h, r: (b * nr + r, h)),
        out_shape=jax.ShapeDtypeStruct((batch * seq, N_HEADS * HEAD_DIM), BF16),
        scratch_shapes=[pltpu.VMEM((HEAD_DIM, HEAD_DIM), F32)],
        compiler_params=_params("parallel", "parallel", "arbitrary"),
        name="hgrn2",
    )(lower_bounds, proj, proj, proj, proj, out_norm.reshape(1, HEAD_DIM))


def _merge_kernel(a0, a1, a2, w0, w1, w2, g0, g1, g2, o_ref, wbf_ref):
    @pl.when(pl.program_id(1) == 0)
    def _():
        for n, w_ref in enumerate((w0, w1, w2)):
            _stage_weight(wbf_ref.at[n], w_ref)

    acc = _sigmoid(g0[...].astype(F32)) * _dot(a0[...], wbf_ref[0])
    acc = acc + _sigmoid(g1[...].astype(F32)) * _dot(a1[...], wbf_ref[1])
    acc = acc + _sigmoid(g2[...].astype(F32)) * _dot(a2[...], wbf_ref[2])
    o_ref[...] = acc.astype(o_ref.dtype)


def gated_merge(o_sb, o_mla, o_hg, w_sb, w_mla, w_hg, layer, gate_logits, tm=1024, tn=512):
    m, k = o_sb.shape
    n = w_sb.shape[2]
    nb = n // tn
    a_spec = pl.BlockSpec((tm, k), lambda j, i: (i, 0))
    w_spec = pl.BlockSpec((None, k, tn), lambda j, i: (layer, 0, j))

    def g_spec(branch):
        return pl.BlockSpec((tm, tn), lambda j, i: (i, branch * nb + j))

    return pl.pallas_call(
        _merge_kernel,
        grid=(nb, m // tm),
        in_specs=[a_spec, a_spec, a_spec, w_spec, w_spec, w_spec, g_spec(0), g_spec(1), g_spec(2)],
        out_specs=pl.BlockSpec((tm, tn), lambda j, i: (i, j)),
        out_shape=jax.ShapeDtypeStruct((m, n), BF16),
        scratch_shapes=[pltpu.VMEM((3, k, tn), BF16)],
        compiler_params=_params("arbitrary", "arbitrary"),
        name="gated_merge",
    )(o_sb, o_mla, o_hg, w_sb, w_mla, w_hg, gate_logits, gate_logits, gate_logits)


HALO = 16

def _up_conv_glu_kernel(a_ref, ah_ref, wg_ref, wv_ref, cg_ref, cv_ref, o_ref, wbf_ref, *, tiles_per_seq):
    @pl.when(pl.program_id(1) == 0)
    def _():
        _stage_weight(wbf_ref.at[0], wg_ref)
        _stage_weight(wbf_ref.at[1], wv_ref)

    a = a_ref[...]
    ah = ah_ref[...]
    seq_start = (pl.program_id(1) % tiles_per_seq) == 0
    row = lax.broadcasted_iota(jnp.int32, o_ref.shape, 0)

    def conv(slot, c_ref):
        w = wbf_ref[slot]
        u = _dot(a, w)
        halo = jnp.where(seq_start, 0.0, _dot(ah, w))
        prev1 = halo[HALO - 1:HALO]
        prev2 = halo[HALO - 2:HALO - 1]
        u1 = jnp.where(row == 0, prev1, pltpu.roll(u, 1, 0))
        u2 = jnp.where(row == 0, prev2, jnp.where(row == 1, prev1, pltpu.roll(u, 2, 0)))
        c = c_ref[...]
        return u2 * c[0:1] + u1 * c[1:2] + u * c[2:3]

    gate = conv(0, cg_ref)
    val = conv(1, cv_ref)
    o_ref[...] = (gate * _sigmoid(gate) * val).astype(o_ref.dtype)


def up_conv_glu(h, w_up, conv_w, layer, seq, tm=1024, tn=256):
    m, k = h.shape
    d_ff = w_up.shape[2] // 2
    nb = d_ff // tn
    kern = functools.partial(_up_conv_glu_kernel, tiles_per_seq=seq // tm)
    halo_blocks = tm // HALO
    return pl.pallas_call(
        kern,
        grid=(nb, m // tm),
        in_specs=[pl.BlockSpec((tm, k), lambda j, i: (i, 0)),
                  pl.BlockSpec((HALO, k), lambda j, i: (jnp.maximum(i * halo_blocks - 1, 0), 0)),
                  pl.BlockSpec((None, k, tn), lambda j, i: (layer, 0, j)),
                  pl.BlockSpec((None, k, tn), lambda j, i: (layer, 0, nb + j)),
                  pl.BlockSpec((None, conv_w.shape[1], tn), lambda j, i: (layer, 0, j)),
                  pl.BlockSpec((None, conv_w.shape[1], tn), lambda j, i: (layer, 0, nb + j))],
        out_specs=pl.BlockSpec((tm, tn), lambda j, i: (i, j)),
        out_shape=jax.ShapeDtypeStruct((m, d_ff), BF16),
        scratch_shapes=[pltpu.VMEM((2, k, tn), BF16)],
        compiler_params=_params("arbitrary", "arbitrary"),
        name="up_conv_glu",
    )(h, h, w_up, w_up, conv_w, conv_w)


def _rope_lane_tables(seq):
    pos = jnp.arange(seq, dtype=F32)
    inv = ROPE_THETA ** (-jnp.arange(0, MLA_ROPE, 2, dtype=F32) / MLA_ROPE)
    ang = pos[:, None] * inv[None, :]
    cos, sin = jnp.cos(ang), jnp.sin(ang)
    zeros = jnp.zeros((seq, HEAD_DIM - MLA_ROPE), F32)
    return (jnp.concatenate([cos, cos, zeros], axis=1),
            jnp.concatenate([-sin, sin, zeros], axis=1))


def _pad_head_gain(g):
    return jnp.pad(g, (0, MLA_PAD - MLA_QK)).reshape(1, MLA_PAD)


def kernel(x, lower_bounds, attn_norm, w_in, sb_q_norm, sb_k_norm, mla_q_a_norm, mla_kv_a_norm, mla_w_q_b, mla_w_kv_b, mla_q_norm, mla_k_norm, hg_out_norm, w_branch_sb, w_branch_mla, w_branch_hg, w_out, ffn_norm, w_up, ffn_conv, w_down):
    batch, seq, d_model = x.shape
    depth = w_in.shape[0]
    width = N_HEADS * HEAD_DIM
    q_lora = mla_w_q_b.shape[1]
    kv_lora = mla_w_kv_b.shape[1]
    c_attn = 3 * width + q_lora + kv_lora
    c_hg = c_attn + MLA_ROPE
    c_gate = c_hg + 4 * width

    cos_t, sin_t = _rope_lane_tables(seq)
    xf = x.reshape(batch * seq, d_model)
    w_in_t = jnp.swapaxes(w_in, 1, 2)
    for l in range(depth):
        wq = jnp.pad(mla_w_q_b[l], ((0, 0), (0, 0), (0, MLA_PAD - MLA_QK))).reshape(q_lora, N_HEADS * MLA_PAD)
        wkn = mla_w_kv_b[l][:, :, :HEAD_DIM].reshape(kv_lora, width)
        wv = mla_w_kv_b[l][:, :, HEAD_DIM:].reshape(kv_lora, width)

        h = rmsnorm_rows(xf, attn_norm[l])
        p_attn = weight_matmul(h, w_in_t, l, BF16, 1024, 512, transposed=True, n=c_attn)
        p_kr = weight_matmul(h, w_in_t, l, F32, 1024, HEAD_DIM, transposed=True, col0=c_attn, n=HEAD_DIM)
        p_hg = weight_matmul(h, w_in_t, l, F32, 1024, 512, transposed=True, col0=c_hg, n=c_gate - c_hg)
        p_gate = weight_matmul(h, w_in_t, l, BF16, 1024, 512, transposed=True, col0=c_gate)

        sb_qk = headnorm_qk(p_attn, jnp.stack([sb_q_norm[l], sb_k_norm[l]]).reshape(2, 1, HEAD_DIM), width)
        o_sb = sb_attention(sb_qk, p_attn, batch, seq)

        mq, mk, mv = mla_prep(p_attn, p_kr, mla_q_a_norm[l].reshape(1, q_lora),
                              mla_kv_a_norm[l].reshape(1, kv_lora), wq.astype(BF16), wkn.astype(BF16),
                              wv.astype(BF16), _pad_head_gain(mla_q_norm[l]), _pad_head_gain(mla_k_norm[l]),
                              cos_t, sin_t, seq, q_lora, kv_lora)
        o_mla = mla_attention(mq, mk, mv, batch, seq)

        o_hg = hgrn2(p_hg, lower_bounds, hg_out_norm[l], l, batch, seq)

        merged = gated_merge(o_sb, o_mla, o_hg, w_branch_sb, w_branch_mla, w_branch_hg, l, p_gate)
        xf = weight_matmul(merged, w_out, l, F32, 1024, 512, res=xf)

        h2 = rmsnorm_rows(xf, ffn_norm[l])
        act = up_conv_glu(h2, w_up, ffn_conv, l, seq)
        for part in range(2):
            xf = weight_matmul(act, w_down, l, F32, 1024, 512, k_part=(part, 2), res=xf)
    return xf.reshape(batch, seq, d_model)
```

```python
import functools

import jax
import jax.numpy as jnp
from jax import lax
from jax.experimental import pallas as pl
from jax.experimental.pallas import tpu as pltpu

F32 = jnp.float32
BF16 = jnp.bfloat16

SUBLANES = 8
HEAD_DIM = 128
N_HEADS = 8
MLA_ROPE = 64
MLA_QK = HEAD_DIM + MLA_ROPE
MLA_PAD = 256
HG_CHUNK = 64
ROPE_THETA = 10000.0
EPS = 1e-6
MASK_NEG = -1e30
LB_FLOOR = 1e-30

VMEM_LIMIT_BYTES = 56 * 1024 * 1024


def _params(*sem):
    return pltpu.CompilerParams(dimension_semantics=sem, vmem_limit_bytes=VMEM_LIMIT_BYTES)


def _dot(a, b):
    return jnp.dot(a, b, preferred_element_type=F32)


def _dot_nt(a, b):
    return lax.dot_general(a, b, (((1,), (1,)), ((), ())), preferred_element_type=F32)


def _dot_tn(a, b):
    return lax.dot_general(a, b, (((0,), (0,)), ((), ())), preferred_element_type=F32)


def _rms(x, width):
    return x * lax.rsqrt(jnp.sum(x * x, axis=-1, keepdims=True) / width + EPS)


def _sigmoid(x):
    return 0.5 * jnp.tanh(0.5 * x) + 0.5


def _rmsnorm_kernel(x_ref, g_ref, o_ref):
    x = x_ref[...]
    o_ref[...] = (_rms(x, x.shape[-1]) * g_ref[...]).astype(o_ref.dtype)


def rmsnorm_rows(x, g, tm=256):
    m, d = x.shape
    return pl.pallas_call(
        _rmsnorm_kernel,
        grid=(m // tm,),
        in_specs=[pl.BlockSpec((tm, d), lambda i: (i, 0)),
                  pl.BlockSpec((1, d), lambda i: (0, 0))],
        out_specs=pl.BlockSpec((tm, d), lambda i: (i, 0)),
        out_shape=jax.ShapeDtypeStruct((m, d), BF16),
        compiler_params=_params("parallel"),
        name="rmsnorm_rows",
    )(x, g.reshape(1, d))


CAST_ELEMS = 512 * 512


def _stage_weight(dst_ref, w_ref):
    nrows, ncols = dst_ref.shape
    rows = min(nrows, CAST_ELEMS // ncols)

    def body(r, carry):
        sl = pl.ds(pl.multiple_of(r * rows, rows), rows)
        dst_ref[sl, :] = w_ref[sl, :].astype(BF16)
        return carry

    lax.fori_loop(0, nrows // rows, body, 0)


def _wmm_kernel(*refs, transposed, residual):
    a_ref, w_ref = refs[0], refs[1]
    r_ref = refs[2] if residual else None
    o_ref, wbf_ref = refs[-2], refs[-1]

    @pl.when(pl.program_id(1) == 0)
    def _():
        _stage_weight(wbf_ref, w_ref.at[0] if transposed else w_ref)

    d = (_dot_nt if transposed else _dot)(a_ref[...], wbf_ref[...])
    if residual:
        d = r_ref[...] + d
    o_ref[...] = d.astype(o_ref.dtype)


def weight_matmul(a, w, layer, out_dtype, tm, tn, *, transposed=False, col0=0, n=None, k_part=(0, 1), res=None):
    m = a.shape[0]
    kp, parts = k_part
    k = a.shape[1] // parts
    if transposed:
        assert parts == 1 and w.shape[2] == k
        n = w.shape[1] - col0 if n is None else n
        w_spec = pl.BlockSpec((pl.Element(1), pl.Element(tn), pl.Element(k)),
                              lambda j, i: (layer, pl.multiple_of(col0 + j * tn, SUBLANES), 0))
        staged = (tn, k)
    else:
        assert col0 % tn == 0 and w.shape[1] == k * parts
        n = w.shape[2] - col0 if n is None else n
        w_spec = pl.BlockSpec((None, k, tn), lambda j, i: (layer, kp, col0 // tn + j))
        staged = (k, tn)
    in_specs = [pl.BlockSpec((tm, k), lambda j, i: (i, kp)), w_spec]
    args = [a, w]
    if res is not None:
        in_specs.append(pl.BlockSpec((tm, tn), lambda j, i: (i, j)))
        args.append(res)
    return pl.pallas_call(
        functools.partial(_wmm_kernel, transposed=transposed, residual=res is not None),
        grid=(n // tn, m // tm),
        in_specs=in_specs,
        out_specs=pl.BlockSpec((tm, tn), lambda j, i: (i, j)),
        out_shape=jax.ShapeDtypeStruct((m, n), out_dtype),
        scratch_shapes=[pltpu.VMEM(staged, BF16)],
        compiler_params=_params("arbitrary", "arbitrary"),
        name="weight_matmul",
    )(*args)


def _headnorm_kernel(x_ref, g_ref, o_ref):
    g = g_ref[0]
    for h in range(x_ref.shape[1] // HEAD_DIM):
        sl = slice(h * HEAD_DIM, (h + 1) * HEAD_DIM)
        x = x_ref[:, sl].astype(F32)
        o_ref[:, sl] = (_rms(x, HEAD_DIM) * g).astype(o_ref.dtype)


def headnorm_qk(proj, gains, width, tm=512):
    m = proj.shape[0]
    return pl.pallas_call(
        _headnorm_kernel,
        grid=(m // tm, 2),
        in_specs=[pl.BlockSpec((tm, width), lambda i, j: (i, j)),
                  pl.BlockSpec((1, 1, HEAD_DIM), lambda i, j: (j, 0, 0))],
        out_specs=pl.BlockSpec((tm, width), lambda i, j: (i, j)),
        out_shape=jax.ShapeDtypeStruct((m, 2 * width), BF16),
        compiler_params=_params("parallel", "arbitrary"),
        name="headnorm_qk",
    )(proj, gains)


NEVER = -(1 << 30)
SB_STAGES = 5
SB_TILE, SB_BLOCK, SB_THRESHOLD, SB_LAST = 0, 1, 2, 4


def _sb_walk_table(n_tiles, blk):
    idle = (0, 0, NEVER, NEVER, 0, 1.0)
    cols = [idle] * (SB_STAGES - 1)
    for tile in range(n_tiles):
        last = 2 * tile + 1
        for step in range(last + 1):
            cols.append((tile, last - step, (step - 1) * blk, step * blk, int(step == last), float(step > 0)))
    cols += [idle] * (SB_STAGES - 1)
    ints = jnp.array([[col[r] for col in cols] for r in range(5)], jnp.int32)
    keep = jnp.array([col[5] for col in cols], F32)
    return ints, keep


def _sb_attn_kernel(walk_ref, keep_ref, q_ref, k_ref, v_ref, o_ref, z1, hilo, z2, zi, wb, carry, acc, *, blk, scale):
    row = lax.broadcasted_iota(jnp.int32, (blk, blk), 0)
    col = lax.broadcasted_iota(jnp.int32, (blk, blk), 1)
    delta = col - row
    tri = (row >= col).astype(BF16)
    from_here = jnp.concatenate([tri, tri], axis=0)
    for ref in (z1, hilo, z2, zi, wb, carry, acc):
        ref[...] = jnp.zeros_like(ref)

    def rows(ref, block):
        return ref[pl.ds(pl.multiple_of(block * blk, blk), blk), :]

    def trip(t, _):
        p1, p2, p4, p5 = t + 4, t + 3, t + 1, t
        ks = rows(k_ref, walk_ref[SB_BLOCK, p1])
        vs = rows(v_ref, walk_ref[SB_BLOCK, p5])
        for c in range(2):
            acc[c] = acc[c] * keep_ref[p5] + _dot(wb[c], vs)
        for c in range(2):
            seen = delta < walk_ref[SB_THRESHOLD + c, p4]
            wb[c] = jnp.where(seen, jnp.exp(zi[c]), 0.0).astype(BF16)
        for c in range(2):
            zi[c] = z2[c] + _dot(hilo[c], from_here)
        for c in range(2):
            z = z1[c]
            seen = delta < walk_ref[SB_THRESHOLD + c, p2]
            log_1m_beta = -(jnp.maximum(z, 0.0) + jnp.log(1.0 + jnp.exp(-jnp.abs(z))))
            log_1m_beta = jnp.where(seen, log_1m_beta, 0.0)
            hi = log_1m_beta.astype(BF16)
            hilo[c, :, :blk] = hi
            hilo[c, :, blk:] = (log_1m_beta - hi.astype(F32)).astype(BF16)
            later = carry[c] * keep_ref[p2]
            z2[c] = z + jnp.concatenate([later] * (blk // HEAD_DIM), axis=1)
            carry[c] = later + jnp.sum(log_1m_beta, axis=1, keepdims=True)
        for c in range(2):
            z1[c] = _dot_nt(rows(q_ref, 2 * walk_ref[SB_TILE, p1] + c), ks) * scale

        @pl.when(walk_ref[SB_LAST, p5] == 1)
        def _():
            for c in range(2):
                o_ref[pl.ds(pl.multiple_of((2 * walk_ref[SB_TILE, p5] + c) * blk, blk), blk), :] = (
                    acc[c].astype(o_ref.dtype))

        return 0

    lax.fori_loop(0, keep_ref.shape[0] - (SB_STAGES - 1), trip, 0)


def sb_attention(qk, proj, batch, seq, blk=256):
    width = N_HEADS * HEAD_DIM
    walk, keep = _sb_walk_table(seq // (2 * blk), blk)
    kern = functools.partial(_sb_attn_kernel, blk=blk, scale=HEAD_DIM ** -0.5)
    grid_spec = pltpu.PrefetchScalarGridSpec(
        num_scalar_prefetch=2,
        grid=(batch, N_HEADS),
        in_specs=[pl.BlockSpec((seq, HEAD_DIM), lambda b, h, *_: (b, h)),
                  pl.BlockSpec((seq, HEAD_DIM), lambda b, h, *_: (b, N_HEADS + h)),
                  pl.BlockSpec((seq, HEAD_DIM), lambda b, h, *_: (b, 2 * N_HEADS + h))],
        out_specs=pl.BlockSpec((seq, HEAD_DIM), lambda b, h, *_: (b, h)),
        scratch_shapes=[pltpu.VMEM((2, blk, blk), F32),
                        pltpu.VMEM((2, blk, 2 * blk), BF16),
                        pltpu.VMEM((2, blk, blk), F32),
                        pltpu.VMEM((2, blk, blk), F32),
                        pltpu.VMEM((2, blk, blk), BF16),
                        pltpu.VMEM((2, blk, HEAD_DIM), F32),
                        pltpu.VMEM((2, blk, HEAD_DIM), F32)])
    return pl.pallas_call(
        kern,
        grid_spec=grid_spec,
        out_shape=jax.ShapeDtypeStruct((batch * seq, width), BF16),
        compiler_params=_params("parallel", "arbitrary"),
        name="sb_attention",
    )(walk, keep, qk, qk, proj)


def _rope_lanes(x, cos_t, sin_t):
    lane = lax.broadcasted_iota(jnp.int32, x.shape, 1)
    half = MLA_ROPE // 2
    partner = jnp.where(lane < half, pltpu.roll(x, HEAD_DIM - half, 1), pltpu.roll(x, half, 1))
    return x * cos_t + partner * sin_t


def _mla_prep_kernel(p_ref, kr_ref, qa_ref, kva_ref, wq_ref, wkn_ref, wv_ref, gq_ref, gk_ref,
                     cos_ref, sin_ref, q_out, k_out, v_out, *, q_lora):
    cos_t = cos_ref[...]
    sin_t = sin_ref[...]
    gq = gq_ref[...]
    gk = gk_ref[...]
    cq = p_ref[:, :q_lora].astype(F32)
    cqn = (_rms(cq, q_lora) * qa_ref[...]).astype(BF16)
    qraw = _dot(cqn, wq_ref[...])
    for h in range(N_HEADS):
        blk = qraw[:, h * MLA_PAD:(h + 1) * MLA_PAD]
        y = _rms(blk, MLA_QK) * gq
        q_out[:, h * MLA_PAD:h * MLA_PAD + HEAD_DIM] = y[:, :HEAD_DIM].astype(q_out.dtype)
        q_out[:, h * MLA_PAD + HEAD_DIM:(h + 1) * MLA_PAD] = _rope_lanes(
            y[:, HEAD_DIM:], cos_t, sin_t).astype(q_out.dtype)

    ckv = p_ref[:, q_lora:].astype(F32)
    ckvn = (_rms(ckv, ckv.shape[-1]) * kva_ref[...]).astype(BF16)
    kn = _dot(ckvn, wkn_ref[...])
    v_out[...] = _dot(ckvn, wv_ref[...]).astype(v_out.dtype)
    kr = kr_ref[...]
    lane = lax.broadcasted_iota(jnp.int32, kr.shape, 1)
    kr = jnp.where(lane < MLA_ROPE, kr, 0.0)
    kr_ss = jnp.sum(kr * kr, axis=-1, keepdims=True)
    for h in range(N_HEADS):
        kb = kn[:, h * HEAD_DIM:(h + 1) * HEAD_DIM]
        inv = lax.rsqrt((jnp.sum(kb * kb, axis=-1, keepdims=True) + kr_ss) / MLA_QK + EPS)
        k_out[:, h * MLA_PAD:h * MLA_PAD + HEAD_DIM] = (kb * inv * gk[:, :HEAD_DIM]).astype(k_out.dtype)
        k_out[:, h * MLA_PAD + HEAD_DIM:(h + 1) * MLA_PAD] = _rope_lanes(
            kr * inv * gk[:, HEAD_DIM:], cos_t, sin_t).astype(k_out.dtype)


def mla_prep(proj, kr, qa, kva, wq, wkn, wv, gq, gk, cos_t, sin_t, seq, q_lora, kv_lora, tm=512):
    m = proj.shape[0]
    lat = q_lora + kv_lora
    lat_blk = (3 * N_HEADS * HEAD_DIM) // lat
    assert lat_blk * lat == 3 * N_HEADS * HEAD_DIM
    ns = seq // tm
    full = lambda i: (0, 0)
    kern = functools.partial(_mla_prep_kernel, q_lora=q_lora)
    return pl.pallas_call(
        kern,
        grid=(m // tm,),
        in_specs=[pl.BlockSpec((tm, lat), lambda i: (i, lat_blk)),
                  pl.BlockSpec((tm, HEAD_DIM), lambda i: (i, 0)),
                  pl.BlockSpec((1, q_lora), full),
                  pl.BlockSpec((1, kv_lora), full),
                  pl.BlockSpec(wq.shape, full),
                  pl.BlockSpec(wkn.shape, full),
                  pl.BlockSpec(wv.shape, full),
                  pl.BlockSpec((1, MLA_PAD), full),
                  pl.BlockSpec((1, MLA_PAD), full),
                  pl.BlockSpec((tm, HEAD_DIM), lambda i: (i % ns, 0)),
                  pl.BlockSpec((tm, HEAD_DIM), lambda i: (i % ns, 0))],
        out_specs=[pl.BlockSpec((tm, N_HEADS * MLA_PAD), lambda i: (i, 0)),
                   pl.BlockSpec((tm, N_HEADS * MLA_PAD), lambda i: (i, 0)),
                   pl.BlockSpec((tm, N_HEADS * HEAD_DIM), lambda i: (i, 0))],
        out_shape=[jax.ShapeDtypeStruct((m, N_HEADS * MLA_PAD), BF16),
                   jax.ShapeDtypeStruct((m, N_HEADS * MLA_PAD), BF16),
                   jax.ShapeDtypeStruct((m, N_HEADS * HEAD_DIM), BF16)],
        compiler_params=_params("parallel"),
        name="mla_prep",
    )(proj, kr, qa, kva, wq, wkn, wv, gq, gk, cos_t, sin_t)


def _softmax_block(q, ks, vs, state, scale, causal):
    m, l, acc = state
    z = _dot_nt(q, ks)
    if causal is not None:
        z = jnp.where(causal, z, MASK_NEG)
    m_new = jnp.maximum(m, jnp.max(z, axis=1, keepdims=True))
    p = jnp.exp((z - m_new) * scale)
    alpha = jnp.exp((m - m_new) * scale)
    l = alpha * l + jnp.sum(p, axis=1, keepdims=True)
    acc = alpha * acc + _dot(p.astype(BF16), vs)
    return m_new, l, acc


def _mla_attn_kernel(q_ref, k_ref, v_ref, o_ref, *, blk, scale):
    qi = pl.program_id(2)
    qa = q_ref[:blk, :]
    qb = q_ref[blk:, :]
    row = lax.broadcasted_iota(jnp.int32, (blk, 2 * blk), 0)
    col = lax.broadcasted_iota(jnp.int32, (blk, 2 * blk), 1)
    block = functools.partial(_softmax_block, scale=scale)
    init = (jnp.full((blk, 1), MASK_NEG, F32), jnp.zeros((blk, 1), F32), jnp.zeros((blk, HEAD_DIM), F32))

    def kv(j):
        start = pl.multiple_of(j * 2 * blk, 2 * blk)
        return k_ref[pl.ds(start, 2 * blk), :], v_ref[pl.ds(start, 2 * blk), :]

    def body(t, c):
        sa, sb = c
        ks, vs = kv(t)
        return block(qa, ks, vs, sa, causal=None), block(qb, ks, vs, sb, causal=None)

    sa, sb = lax.fori_loop(0, qi, body, (init, init))
    ks, vs = kv(qi)
    sa = block(qa, ks[:blk], vs[:blk], sa, causal=(col <= row)[:, :blk])
    sb = block(qb, ks, vs, sb, causal=col <= row + blk)
    o_ref[:blk, :] = (sa[2] / sa[1]).astype(o_ref.dtype)
    o_ref[blk:, :] = (sb[2] / sb[1]).astype(o_ref.dtype)


def mla_attention(q, k, v, batch, seq, blk=1024):
    nq = seq // (2 * blk)
    kern = functools.partial(_mla_attn_kernel, blk=blk, scale=MLA_QK ** -0.5)
    return pl.pallas_call(
        kern,
        grid=(batch, N_HEADS, nq),
        in_specs=[pl.BlockSpec((2 * blk, MLA_PAD), lambda b, h, i: (b * nq + i, h)),
                  pl.BlockSpec((seq, MLA_PAD), lambda b, h, i: (b, h)),
                  pl.BlockSpec((seq, HEAD_DIM), lambda b, h, i: (b, h))],
        out_specs=pl.BlockSpec((2 * blk, HEAD_DIM), lambda b, h, i: (b * nq + i, h)),
        out_shape=jax.ShapeDtypeStruct((batch * seq, N_HEADS * HEAD_DIM), BF16),
        compiler_params=_params("parallel", "parallel", "arbitrary"),
        name="mla_attention",
    )(q, k, v)


def _seg_cumsum(x, seg, row):
    pos = row & (seg - 1)
    s = 1
    while s < seg:
        x = x + jnp.where(pos >= s, pltpu.roll(x, s, 0), 0.0)
        s *= 2
    return x


def _seg_rev_cumsum_excl(x, seg, row):
    pos = row & (seg - 1)
    n = x.shape[0]
    y = x
    s = 1
    while s < seg:
        y = y + jnp.where(pos < seg - s, pltpu.roll(y, n - s, 0), 0.0)
        s *= 2
    return y - x


def _rebase_halves(x, half, second):
    pieces = []
    for s in range(0, x.shape[0], 2 * half):
        pivot = x[s + half - 1:s + half]
        lo, hi = x[s:s + half], x[s + half:s + 2 * half]
        pieces += [lo, hi - pivot] if second else [lo - pivot, hi]
    return jnp.concatenate(pieces, axis=0)


def _hgrn2_kernel(lb_ref, q_ref, z_ref, i_ref, g_ref, gn_ref, o_ref, st_ref, *, rows, layer):
    @pl.when(pl.program_id(2) == 0)
    def _():
        st_ref[...] = jnp.zeros_like(st_ref)

    lbp = lb_ref[...]
    e = jnp.exp(lbp - jnp.max(lbp, axis=0, keepdims=True))
    p = e / jnp.sum(e, axis=0, keepdims=True)
    lb = jnp.sum(p[:layer + 1], axis=0, keepdims=True) - p[0:1]

    z = z_ref[...]
    q = q_ref[...]
    v = i_ref[...]
    log_sig = jnp.minimum(z, 0.0) - jnp.log(1.0 + jnp.exp(-jnp.abs(z)))
    a = jnp.log(jnp.maximum(lb, LB_FLOOR))
    c = jnp.log1p(-lb) + log_sig
    log_f = jnp.maximum(a, c) + jnp.log(1.0 + jnp.exp(-jnp.abs(a - c)))
    k = 1.0 - jnp.exp(log_f)

    row = lax.broadcasted_iota(jnp.int32, (rows, HEAD_DIM), 0)
    fwd = {HG_CHUNK: _seg_cumsum(log_f, HG_CHUNK, row)}
    rev = {HG_CHUNK: _seg_rev_cumsum_excl(log_f, HG_CHUNK, row)}
    half = HG_CHUNK // 2
    while half >= 1:
        if half >= SUBLANES:
            fwd[half] = _rebase_halves(fwd[2 * half], half, second=True)
            rev[half] = _rebase_halves(rev[2 * half], half, second=False)
        else:
            fwd[half] = _seg_cumsum(log_f, half, row)
            rev[half] = _seg_rev_cumsum_excl(log_f, half, row)
        half //= 2
    b = fwd[HG_CHUNK]
    q_in = (q * jnp.exp(b)).astype(BF16)
    k_out = (k * jnp.exp(rev[HG_CHUNK])).astype(BF16)
    v16 = v.astype(BF16)
    diag = jnp.sum(q * k, axis=1, keepdims=True)

    levels = []
    half = 1
    while half < HG_CHUNK:
        second = (row & (2 * half - 1)) >= half
        ql = jnp.where(second, q * jnp.exp(fwd[half]), 0.0).astype(BF16)
        kl = jnp.where(second, 0.0, k * jnp.exp(rev[half])).astype(BF16)
        levels.append((2 * half, ql, kl))
        half *= 2

    ct = lax.broadcasted_iota(jnp.int32, (HG_CHUNK, HG_CHUNK), 0)
    cs = lax.broadcasted_iota(jnp.int32, (HG_CHUNK, HG_CHUNK), 1)
    st = st_ref[...]
    outs = []
    for ci in range(rows // HG_CHUNK):
        sl = slice(ci * HG_CHUNK, (ci + 1) * HG_CHUNK)
        attn = jnp.zeros((HG_CHUNK, HG_CHUNK), F32)
        for w, ql, kl in levels:
            part = _dot_nt(ql[sl], kl[sl])
            if w < HG_CHUNK:
                part = jnp.where((ct & -w) == (cs & -w), part, 0.0)
            attn = attn + part
        o = _dot(attn.astype(BF16), v16[sl]) + diag[sl] * v[sl] + _dot_nt(q_in[sl], st.astype(BF16))
        outs.append(o)
        b_last = b[(ci + 1) * HG_CHUNK - 1:(ci + 1) * HG_CHUNK, :]
        st = st * jnp.exp(b_last) + _dot_tn(v16[sl], k_out[sl])
    st_ref[...] = st

    o = jnp.concatenate(outs, axis=0)
    g = g_ref[...]
    o_ref[...] = (_rms(o, HEAD_DIM) * gn_ref[...] * (g * _sigmoid(g))).astype(o_ref.dtype)


def hgrn2(proj, lower_bounds, out_norm, layer, batch, seq, rows=1024):
    depth = lower_bounds.shape[0]
    nr = seq // rows
    kern = functools.partial(_hgrn2_kernel, rows=rows, layer=layer)

    def col(c):
        return pl.BlockSpec((rows, HEAD_DIM), lambda b, h, r: (b * nr + r, c * N_HEADS + h))

    return pl.pallas_call(
        kern,
        grid=(batch, N_HEADS, nr),
        in_specs=[pl.BlockSpec((depth, HEAD_DIM), lambda b, h, r: (0, h)),
                  col(0), col(1), col(2), col(3),
                  pl.BlockSpec((1, HEAD_DIM), lambda b, h, r: (0, 0))],
        out_specs=pl.BlockSpec((rows, HEAD_DIM), lambda b, h, r: (b * nr + r, h)),
        out_shape=jax.ShapeDtypeStruct((batch * seq, N_HEADS * HEAD_DIM), BF16),
        scratch_shapes=[pltpu.VMEM((HEAD_DIM, HEAD_DIM), F32)],
        compiler_params=_params("parallel", "parallel", "arbitrary"),
        name="hgrn2",
    )(lower_bounds, proj, proj, proj, proj, out_norm.reshape(1, HEAD_DIM))


def _merge_kernel(a0, a1, a2, w0, w1, w2, g0, g1, g2, o_ref, wbf_ref):
    @pl.when(pl.program_id(1) == 0)
    def _():
        for n, w_ref in enumerate((w0, w1, w2)):
            _stage_weight(wbf_ref.at[n], w_ref)

    acc = _sigmoid(g0[...].astype(F32)) * _dot(a0[...], wbf_ref[0])
    acc = acc + _sigmoid(g1[...].astype(F32)) * _dot(a1[...], wbf_ref[1])
    acc = acc + _sigmoid(g2[...].astype(F32)) * _dot(a2[...], wbf_ref[2])
    o_ref[...] = acc.astype(o_ref.dtype)


def gated_merge(o_sb, o_mla, o_hg, w_sb, w_mla, w_hg, layer, gate_logits, tm=1024, tn=512):
    m, k = o_sb.shape
    n = w_sb.shape[2]
    nb = n // tn
    a_spec = pl.BlockSpec((tm, k), lambda j, i: (i, 0))
    w_spec = pl.BlockSpec((None, k, tn), lambda j, i: (layer, 0, j))

    def g_spec(branch):
        return pl.BlockSpec((tm, tn), lambda j, i: (i, branch * nb + j))

    return pl.pallas_call(
        _merge_kernel,
        grid=(nb, m // tm),
        in_specs=[a_spec, a_spec, a_spec, w_spec, w_spec, w_spec, g_spec(0), g_spec(1), g_spec(2)],
        out_specs=pl.BlockSpec((tm, tn), lambda j, i: (i, j)),
        out_shape=jax.ShapeDtypeStruct((m, n), BF16),
        scratch_shapes=[pltpu.VMEM((3, k, tn), BF16)],
        compiler_params=_params("arbitrary", "arbitrary"),
        name="gated_merge",
    )(o_sb, o_mla, o_hg, w_sb, w_mla, w_hg, gate_logits, gate_logits, gate_logits)


HALO = 16

def _up_conv_glu_kernel(a_ref, ah_ref, wg_ref, wv_ref, cg_ref, cv_ref, o_ref, wbf_ref, *, tiles_per_seq):
    @pl.when(pl.program_id(1) == 0)
    def _():
        _stage_weight(wbf_ref.at[0], wg_ref)
        _stage_weight(wbf_ref.at[1], wv_ref)

    a = a_ref[...]
    ah = ah_ref[...]
    seq_start = (pl.program_id(1) % tiles_per_seq) == 0
    row = lax.broadcasted_iota(jnp.int32, o_ref.shape, 0)

    def conv(slot, c_ref):
        w = wbf_ref[slot]
        u = _dot(a, w)
        halo = jnp.where(seq_start, 0.0, _dot(ah, w))
        prev1 = halo[HALO - 1:HALO]
        prev2 = halo[HALO - 2:HALO - 1]
        u1 = jnp.where(row == 0, prev1, pltpu.roll(u, 1, 0))
        u2 = jnp.where(row == 0, prev2, jnp.where(row == 1, prev1, pltpu.roll(u, 2, 0)))
        c = c_ref[...]
        return u2 * c[0:1] + u1 * c[1:2] + u * c[2:3]

    gate = conv(0, cg_ref)
    val = conv(1, cv_ref)
    o_ref[...] = (gate * _sigmoid(gate) * val).astype(o_ref.dtype)


def up_conv_glu(h, w_up, conv_w, layer, seq, tm=1024, tn=256):
    m, k = h.shape
    d_ff = w_up.shape[2] // 2
    nb = d_ff // tn
    kern = functools.partial(_up_conv_glu_kernel, tiles_per_seq=seq // tm)
    halo_blocks = tm // HALO
    return pl.pallas_call(
        kern,
        grid=(nb, m // tm),
        in_specs=[pl.BlockSpec((tm, k), lambda j, i: (i, 0)),
                  pl.BlockSpec((HALO, k), lambda j, i: (jnp.maximum(i * halo_blocks - 1, 0), 0)),
                  pl.BlockSpec((None, k, tn), lambda j, i: (layer, 0, j)),
                  pl.BlockSpec((None, k, tn), lambda j, i: (layer, 0, nb + j)),
                  pl.BlockSpec((None, conv_w.shape[1], tn), lambda j, i: (layer, 0, j)),
                  pl.BlockSpec((None, conv_w.shape[1], tn), lambda j, i: (layer, 0, nb + j))],
        out_specs=pl.BlockSpec((tm, tn), lambda j, i: (i, j)),
        out_shape=jax.ShapeDtypeStruct((m, d_ff), BF16),
        scratch_shapes=[pltpu.VMEM((2, k, tn), BF16)],
        compiler_params=_params("arbitrary", "arbitrary"),
        name="up_conv_glu",
    )(h, h, w_up, w_up, conv_w, conv_w)


def _rope_lane_tables(seq):
    pos = jnp.arange(seq, dtype=F32)
    inv = ROPE_THETA ** (-jnp.arange(0, MLA_ROPE, 2, dtype=F32) / MLA_ROPE)
    ang = pos[:, None] * inv[None, :]
    cos, sin = jnp.cos(ang), jnp.sin(ang)
    zeros = jnp.zeros((seq, HEAD_DIM - MLA_ROPE), F32)
    return (jnp.concatenate([cos, cos, zeros], axis=1),
            jnp.concatenate([-sin, sin, zeros], axis=1))


def _pad_head_gain(g):
    return jnp.pad(g, (0, MLA_PAD - MLA_QK)).reshape(1, MLA_PAD)


def kernel(x, lower_bounds, attn_norm, w_in, sb_q_norm, sb_k_norm, mla_q_a_norm, mla_kv_a_norm, mla_w_q_b, mla_w_kv_b, mla_q_norm, mla_k_norm, hg_out_norm, w_branch_sb, w_branch_mla, w_branch_hg, w_out, ffn_norm, w_up, ffn_conv, w_down):
    batch, seq, d_model = x.shape
    depth = w_in.shape[0]
    width = N_HEADS * HEAD_DIM
    q_lora = mla_w_q_b.shape[1]
    kv_lora = mla_w_kv_b.shape[1]
    c_attn = 3 * width + q_lora + kv_lora
    c_hg = c_attn + MLA_ROPE
    c_gate = c_hg + 4 * width

    cos_t, sin_t = _rope_lane_tables(seq)
    xf = x.reshape(batch * seq, d_model)
    w_in_t = jnp.swapaxes(w_in, 1, 2)
    for l in range(depth):
        wq = jnp.pad(mla_w_q_b[l], ((0, 0), (0, 0), (0, MLA_PAD - MLA_QK))).reshape(q_lora, N_HEADS * MLA_PAD)
        wkn = mla_w_kv_b[l][:, :, :HEAD_DIM].reshape(kv_lora, width)
        wv = mla_w_kv_b[l][:, :, HEAD_DIM:].reshape(kv_lora, width)

        h = rmsnorm_rows(xf, attn_norm[l])
        p_attn = weight_matmul(h, w_in_t, l, BF16, 1024, 512, transposed=True, n=c_attn)
        p_kr = weight_matmul(h, w_in_t, l, F32, 1024, HEAD_DIM, transposed=True, col0=c_attn, n=HEAD_DIM)
        p_hg = weight_matmul(h, w_in_t, l, F32, 1024, 512, transposed=True, col0=c_hg, n=c_gate - c_hg)
        p_gate = weight_matmul(h, w_in_t, l, BF16, 1024, 512, transposed=True, col0=c_gate)

        sb_qk = headnorm_qk(p_attn, jnp.stack([sb_q_norm[l], sb_k_norm[l]]).reshape(2, 1, HEAD_DIM), width)
        o_sb = sb_attention(sb_qk, p_attn, batch, seq)

        mq, mk, mv = mla_prep(p_attn, p_kr, mla_q_a_norm[l].reshape(1, q_lora),
                              mla_kv_a_norm[l].reshape(1, kv_lora), wq.astype(BF16), wkn.astype(BF16),
                              wv.astype(BF16), _pad_head_gain(mla_q_norm[l]), _pad_head_gain(mla_k_norm[l]),
                              cos_t, sin_t, seq, q_lora, kv_lora)
        o_mla = mla_attention(mq, mk, mv, batch, seq)

        o_hg = hgrn2(p_hg, lower_bounds, hg_out_norm[l], l, batch, seq)

        merged = gated_merge(o_sb, o_mla, o_hg, w_branch_sb, w_branch_mla, w_branch_hg, l, p_gate)
        xf = weight_matmul(merged, w_out, l, F32, 1024, 512, res=xf)

        h2 = rmsnorm_rows(xf, ffn_norm[l])
        act = up_conv_glu(h2, w_up, ffn_conv, l, seq)
        for part in range(2):
            xf = weight_matmul(act, w_down, l, F32, 1024, 512, k_part=(part, 2), res=xf)
    return xf.reshape(batch, seq, d_model)
```

```python
import functools

import jax
import jax.numpy as jnp
from jax import lax
from jax.experimental import pallas as pl
from jax.experimental.pallas import tpu as pltpu

F32 = jnp.float32
BF16 = jnp.bfloat16

SUBLANES = 8
HEAD_DIM = 128
N_HEADS = 8
MLA_ROPE = 64
MLA_QK = HEAD_DIM + MLA_ROPE
MLA_PAD = 256
HG_CHUNK = 64
ROPE_THETA = 10000.0
EPS = 1e-6
MASK_NEG = -1e30
LB_FLOOR = 1e-30

VMEM_LIMIT_BYTES = 56 * 1024 * 1024


def _params(*sem):
    return pltpu.CompilerParams(dimension_semantics=sem, vmem_limit_bytes=VMEM_LIMIT_BYTES)


def _dot(a, b):
    return jnp.dot(a, b, preferred_element_type=F32)


def _dot_nt(a, b):
    return lax.dot_general(a, b, (((1,), (1,)), ((), ())), preferred_element_type=F32)


def _dot_tn(a, b):
    return lax.dot_general(a, b, (((0,), (0,)), ((), ())), preferred_element_type=F32)


def _rms(x, width):
    return x * lax.rsqrt(jnp.sum(x * x, axis=-1, keepdims=True) / width + EPS)


def _sigmoid(x):
    return 0.5 * jnp.tanh(0.5 * x) + 0.5


def _rmsnorm_kernel(x_ref, g_ref, o_ref):
    x = x_ref[...]
    o_ref[...] = (_rms(x, x.shape[-1]) * g_ref[...]).astype(o_ref.dtype)


def rmsnorm_rows(x, g, tm=256):
    m, d = x.shape
    return pl.pallas_call(
        _rmsnorm_kernel,
        grid=(m // tm,),
        in_specs=[pl.BlockSpec((tm, d), lambda i: (i, 0)),
                  pl.BlockSpec((1, d), lambda i: (0, 0))],
        out_specs=pl.BlockSpec((tm, d), lambda i: (i, 0)),
        out_shape=jax.ShapeDtypeStruct((m, d), BF16),
        compiler_params=_params("parallel"),
        name="rmsnorm_rows",
    )(x, g.reshape(1, d))


CAST_ELEMS = 512 * 512


def _stage_weight(dst_ref, w_ref):
    nrows, ncols = dst_ref.shape
    rows = min(nrows, CAST_ELEMS // ncols)

    def body(r, carry):
        sl = pl.ds(pl.multiple_of(r * rows, rows), rows)
        dst_ref[sl, :] = w_ref[sl, :].astype(BF16)
        return carry

    lax.fori_loop(0, nrows // rows, body, 0)


def _wmm_kernel(*refs, transposed, residual):
    a_ref, w_ref = refs[0], refs[1]
    r_ref = refs[2] if residual else None
    o_ref, wbf_ref = refs[-2], refs[-1]

    @pl.when(pl.program_id(1) == 0)
    def _():
        _stage_weight(wbf_ref, w_ref.at[0] if transposed else w_ref)

    d = (_dot_nt if transposed else _dot)(a_ref[...], wbf_ref[...])
    if residual:
        d = r_ref[...] + d
    o_ref[...] = d.astype(o_ref.dtype)


def weight_matmul(a, w, layer, out_dtype, tm, tn, *, transposed=False, col0=0, n=None, k_part=(0, 1), res=None):
    m = a.shape[0]
    kp, parts = k_part
    k = a.shape[1] // parts
    if transposed:
        assert parts == 1 and w.shape[2] == k
        n = w.shape[1] - col0 if n is None else n
        w_spec = pl.BlockSpec((pl.Element(1), pl.Element(tn), pl.Element(k)),
                              lambda j, i: (layer, pl.multiple_of(col0 + j * tn, SUBLANES), 0))
        staged = (tn, k)
    else:
        assert col0 % tn == 0 and w.shape[1] == k * parts
        n = w.shape[2] - col0 if n is None else n
        w_spec = pl.BlockSpec((None, k, tn), lambda j, i: (layer, kp, col0 // tn + j))
        staged = (k, tn)
    in_specs = [pl.BlockSpec((tm, k), lambda j, i: (i, kp)), w_spec]
    args = [a, w]
    if res is not None:
        in_specs.append(pl.BlockSpec((tm, tn), lambda j, i: (i, j)))
        args.append(res)
    return pl.pallas_call(
        functools.partial(_wmm_kernel, transposed=transposed, residual=res is not None),
        grid=(n // tn, m // tm),
        in_specs=in_specs,
        out_specs=pl.BlockSpec((tm, tn), lambda j, i: (i, j)),
        out_shape=jax.ShapeDtypeStruct((m, n), out_dtype),
        scratch_shapes=[pltpu.VMEM(staged, BF16)],
        compiler_params=_params("arbitrary", "arbitrary"),
        name="weight_matmul",
    )(*args)


def _headnorm_kernel(x_ref, g_ref, o_ref):
    g = g_ref[0]
    for h in range(x_ref.shape[1] // HEAD_DIM):
        sl = slice(h * HEAD_DIM, (h + 1) * HEAD_DIM)
        x = x_ref[:, sl].astype(F32)
        o_ref[:, sl] = (_rms(x, HEAD_DIM) * g).astype(o_ref.dtype)


def headnorm_qk(proj, gains, width, tm=512):
    m = proj.shape[0]
    return pl.pallas_call(
        _headnorm_kernel,
        grid=(m // tm, 2),
        in_specs=[pl.BlockSpec((tm, width), lambda i, j: (i, j)),
                  pl.BlockSpec((1, 1, HEAD_DIM), lambda i, j: (j, 0, 0))],
        out_specs=pl.BlockSpec((tm, width), lambda i, j: (i, j)),
        out_shape=jax.ShapeDtypeStruct((m, 2 * width), BF16),
        compiler_params=_params("parallel", "arbitrary"),
        name="headnorm_qk",
    )(proj, gains)


def _sb_block(q, ks, vs, carry, acc, from_here, scale, strict):
    z = _dot_nt(q, ks) * scale
    log_1m_beta = -(jnp.maximum(z, 0.0) + jnp.log(1.0 + jnp.exp(-jnp.abs(z))))
    if strict is not None:
        log_1m_beta = jnp.where(strict, log_1m_beta, 0.0)
    hi = log_1m_beta.astype(BF16)
    lo = (log_1m_beta - hi.astype(F32)).astype(BF16)
    incl = _dot(jnp.concatenate([hi, lo], axis=1), from_here)
    w = jnp.exp(z + incl + carry)
    if strict is not None:
        w = jnp.where(strict, w, 0.0)
    acc = acc + _dot(w.astype(BF16), vs)
    carry = carry + jnp.sum(log_1m_beta, axis=1, keepdims=True)
    return carry, acc


def _sb_attn_kernel(q_ref, k_ref, v_ref, o_ref, *, rows, kb, scale):
    qi = pl.program_id(2)
    per = rows // kb
    row = lax.broadcasted_iota(jnp.int32, (rows, kb), 0)
    col = lax.broadcasted_iota(jnp.int32, (rows, kb), 1)
    later = lax.broadcasted_iota(jnp.int32, (kb, kb), 0) >= lax.broadcasted_iota(jnp.int32, (kb, kb), 1)
    tri = later.astype(BF16)
    block = functools.partial(_sb_block, from_here=jnp.concatenate([tri, tri], axis=0), scale=scale)
    halves = (q_ref[:rows, :], q_ref[rows:, :])

    def kv(j):
        start = pl.multiple_of(j * kb, kb)
        return k_ref[pl.ds(start, kb), :], v_ref[pl.ds(start, kb), :]

    first_block = qi * 2 * per
    state = [(jnp.zeros((rows, 1), F32), jnp.zeros((rows, HEAD_DIM), F32)) for _ in halves]
    for d in reversed(range(2 * per)):
        ks, vs = kv(first_block + d)
        for c, q in enumerate(halves):
            start = c * rows
            if d * kb >= start + rows:
                continue
            strict = None if (d + 1) * kb <= start else (col + d * kb) < (row + start)
            state[c] = block(q, ks, vs, *state[c], strict=strict)

    def body(t, st):
        sa, sb = st
        for u in range(2):
            ks, vs = kv(first_block - 1 - 2 * t - u)
            sa = block(halves[0], ks, vs, *sa, strict=None)
            sb = block(halves[1], ks, vs, *sb, strict=None)
        return sa, sb

    sa, sb = lax.fori_loop(0, qi * per, body, tuple(state))
    o_ref[:rows, :] = sa[1].astype(o_ref.dtype)
    o_ref[rows:, :] = sb[1].astype(o_ref.dtype)


def sb_attention(qk, proj, batch, seq, rows=1024, kb=256):
    width = N_HEADS * HEAD_DIM
    nq = seq // (2 * rows)
    kern = functools.partial(_sb_attn_kernel, rows=rows, kb=kb, scale=HEAD_DIM ** -0.5)
    return pl.pallas_call(
        kern,
        grid=(batch, N_HEADS, nq),
        in_specs=[pl.BlockSpec((2 * rows, HEAD_DIM), lambda b, h, i: (b * nq + i, h)),
                  pl.BlockSpec((seq, HEAD_DIM), lambda b, h, i: (b, N_HEADS + h)),
                  pl.BlockSpec((seq, HEAD_DIM), lambda b, h, i: (b, 2 * N_HEADS + h))],
        out_specs=pl.BlockSpec((2 * rows, HEAD_DIM), lambda b, h, i: (b * nq + i, h)),
        out_shape=jax.ShapeDtypeStruct((batch * seq, width), BF16),
        compiler_params=_params("parallel", "parallel", "arbitrary"),
        name="sb_attention",
    )(qk, qk, proj)


def _rope_lanes(x, cos_t, sin_t):
    lane = lax.broadcasted_iota(jnp.int32, x.shape, 1)
    half = MLA_ROPE // 2
    partner = jnp.where(lane < half, pltpu.roll(x, HEAD_DIM - half, 1), pltpu.roll(x, half, 1))
    return x * cos_t + partner * sin_t


def _mla_prep_kernel(p_ref, kr_ref, qa_ref, kva_ref, wq_ref, wkn_ref, wv_ref, gq_ref, gk_ref,
                     cos_ref, sin_ref, q_out, k_out, v_out, *, q_lora):
    cos_t = cos_ref[...]
    sin_t = sin_ref[...]
    gq = gq_ref[...]
    gk = gk_ref[...]
    cq = p_ref[:, :q_lora].astype(F32)
    cqn = (_rms(cq, q_lora) * qa_ref[...]).astype(BF16)
    qraw = _dot(cqn, wq_ref[...])
    for h in range(N_HEADS):
        blk = qraw[:, h * MLA_PAD:(h + 1) * MLA_PAD]
        y = _rms(blk, MLA_QK) * gq
        q_out[:, h * MLA_PAD:h * MLA_PAD + HEAD_DIM] = y[:, :HEAD_DIM].astype(q_out.dtype)
        q_out[:, h * MLA_PAD + HEAD_DIM:(h + 1) * MLA_PAD] = _rope_lanes(
            y[:, HEAD_DIM:], cos_t, sin_t).astype(q_out.dtype)

    ckv = p_ref[:, q_lora:].astype(F32)
    ckvn = (_rms(ckv, ckv.shape[-1]) * kva_ref[...]).astype(BF16)
    kn = _dot(ckvn, wkn_ref[...])
    v_out[...] = _dot(ckvn, wv_ref[...]).astype(v_out.dtype)
    kr = kr_ref[...]
    lane = lax.broadcasted_iota(jnp.int32, kr.shape, 1)
    kr = jnp.where(lane < MLA_ROPE, kr, 0.0)
    kr_ss = jnp.sum(kr * kr, axis=-1, keepdims=True)
    for h in range(N_HEADS):
        kb = kn[:, h * HEAD_DIM:(h + 1) * HEAD_DIM]
        inv = lax.rsqrt((jnp.sum(kb * kb, axis=-1, keepdims=True) + kr_ss) / MLA_QK + EPS)
        k_out[:, h * MLA_PAD:h * MLA_PAD + HEAD_DIM] = (kb * inv * gk[:, :HEAD_DIM]).astype(k_out.dtype)
        k_out[:, h * MLA_PAD + HEAD_DIM:(h + 1) * MLA_PAD] = _rope_lanes(
            kr * inv * gk[:, HEAD_DIM:], cos_t, sin_t).astype(k_out.dtype)


def mla_prep(proj, kr, qa, kva, wq, wkn, wv, gq, gk, cos_t, sin_t, seq, q_lora, kv_lora, tm=512):
    m = proj.shape[0]
    lat = q_lora + kv_lora
    lat_blk = (3 * N_HEADS * HEAD_DIM) // lat
    assert lat_blk * lat == 3 * N_HEADS * HEAD_DIM
    ns = seq // tm
    full = lambda i: (0, 0)
    kern = functools.partial(_mla_prep_kernel, q_lora=q_lora)
    return pl.pallas_call(
        kern,
        grid=(m // tm,),
        in_specs=[pl.BlockSpec((tm, lat), lambda i: (i, lat_blk)),
                  pl.BlockSpec((tm, HEAD_DIM), lambda i: (i, 0)),
                  pl.BlockSpec((1, q_lora), full),
                  pl.BlockSpec((1, kv_lora), full),
                  pl.BlockSpec(wq.shape, full),
                  pl.BlockSpec(wkn.shape, full),
                  pl.BlockSpec(wv.shape, full),
                  pl.BlockSpec((1, MLA_PAD), full),
                  pl.BlockSpec((1, MLA_PAD), full),
                  pl.BlockSpec((tm, HEAD_DIM), lambda i: (i % ns, 0)),
                  pl.BlockSpec((tm, HEAD_DIM), lambda i: (i % ns, 0))],
        out_specs=[pl.BlockSpec((tm, N_HEADS * MLA_PAD), lambda i: (i, 0)),
                   pl.BlockSpec((tm, N_HEADS * MLA_PAD), lambda i: (i, 0)),
                   pl.BlockSpec((tm, N_HEADS * HEAD_DIM), lambda i: (i, 0))],
        out_shape=[jax.ShapeDtypeStruct((m, N_HEADS * MLA_PAD), BF16),
                   jax.ShapeDtypeStruct((m, N_HEADS * MLA_PAD), BF16),
                   jax.ShapeDtypeStruct((m, N_HEADS * HEAD_DIM), BF16)],
        compiler_params=_params("parallel"),
        name="mla_prep",
    )(proj, kr, qa, kva, wq, wkn, wv, gq, gk, cos_t, sin_t)


def _softmax_block(q, ks, vs, state, scale, causal):
    m, l, acc = state
    z = _dot_nt(q, ks)
    if causal is not None:
        z = jnp.where(causal, z, MASK_NEG)
    m_new = jnp.maximum(m, jnp.max(z, axis=1, keepdims=True))
    p = jnp.exp((z - m_new) * scale)
    alpha = jnp.exp((m - m_new) * scale)
    l = alpha * l + jnp.sum(p, axis=1, keepdims=True)
    acc = alpha * acc + _dot(p.astype(BF16), vs)
    return m_new, l, acc


def _mla_attn_kernel(q_ref, k_ref, v_ref, o_ref, *, blk, scale):
    qi = pl.program_id(2)
    qa = q_ref[:blk, :]
    qb = q_ref[blk:, :]
    row = lax.broadcasted_iota(jnp.int32, (blk, 2 * blk), 0)
    col = lax.broadcasted_iota(jnp.int32, (blk, 2 * blk), 1)
    block = functools.partial(_softmax_block, scale=scale)
    init = (jnp.full((blk, 1), MASK_NEG, F32), jnp.zeros((blk, 1), F32), jnp.zeros((blk, HEAD_DIM), F32))

    def kv(j):
        start = pl.multiple_of(j * 2 * blk, 2 * blk)
        return k_ref[pl.ds(start, 2 * blk), :], v_ref[pl.ds(start, 2 * blk), :]

    def body(t, c):
        sa, sb = c
        ks, vs = kv(t)
        return block(qa, ks, vs, sa, causal=None), block(qb, ks, vs, sb, causal=None)

    sa, sb = lax.fori_loop(0, qi, body, (init, init))
    ks, vs = kv(qi)
    sa = block(qa, ks[:blk], vs[:blk], sa, causal=(col <= row)[:, :blk])
    sb = block(qb, ks, vs, sb, causal=col <= row + blk)
    o_ref[:blk, :] = (sa[2] / sa[1]).astype(o_ref.dtype)
    o_ref[blk:, :] = (sb[2] / sb[1]).astype(o_ref.dtype)


def mla_attention(q, k, v, batch, seq, blk=1024):
    nq = seq // (2 * blk)
    kern = functools.partial(_mla_attn_kernel, blk=blk, scale=MLA_QK ** -0.5)
    return pl.pallas_call(
        kern,
        grid=(batch, N_HEADS, nq),
        in_specs=[pl.BlockSpec((2 * blk, MLA_PAD), lambda b, h, i: (b * nq + i, h)),
                  pl.BlockSpec((seq, MLA_PAD), lambda b, h, i: (b, h)),
                  pl.BlockSpec((seq, HEAD_DIM), lambda b, h, i: (b, h))],
        out_specs=pl.BlockSpec((2 * blk, HEAD_DIM), lambda b, h, i: (b * nq + i, h)),
        out_shape=jax.ShapeDtypeStruct((batch * seq, N_HEADS * HEAD_DIM), BF16),
        compiler_params=_params("parallel", "parallel", "arbitrary"),
        name="mla_attention",
    )(q, k, v)


def _seg_cumsum(x, seg, row):
    pos = row & (seg - 1)
    s = 1
    while s < seg:
        x = x + jnp.where(pos >= s, pltpu.roll(x, s, 0), 0.0)
        s *= 2
    return x


def _seg_rev_cumsum_excl(x, seg, row):
    pos = row & (seg - 1)
    n = x.shape[0]
    y = x
    s = 1
    while s < seg:
        y = y + jnp.where(pos < seg - s, pltpu.roll(y, n - s, 0), 0.0)
        s *= 2
    return y - x


def _rebase_halves(x, half, second):
    pieces = []
    for s in range(0, x.shape[0], 2 * half):
        pivot = x[s + half - 1:s + half]
        lo, hi = x[s:s + half], x[s + half:s + 2 * half]
        pieces += [lo, hi - pivot] if second else [lo - pivot, hi]
    return jnp.concatenate(pieces, axis=0)


def _hgrn2_kernel(lb_ref, q_ref, z_ref, i_ref, g_ref, gn_ref, o_ref, st_ref, *, rows, layer):
    @pl.when(pl.program_id(2) == 0)
    def _():
        st_ref[...] = jnp.zeros_like(st_ref)

    lbp = lb_ref[...]
    e = jnp.exp(lbp - jnp.max(lbp, axis=0, keepdims=True))
    p = e / jnp.sum(e, axis=0, keepdims=True)
    lb = jnp.sum(p[:layer + 1], axis=0, keepdims=True) - p[0:1]

    z = z_ref[...]
    q = q_ref[...]
    v = i_ref[...]
    log_sig = jnp.minimum(z, 0.0) - jnp.log(1.0 + jnp.exp(-jnp.abs(z)))
    a = jnp.log(jnp.maximum(lb, LB_FLOOR))
    c = jnp.log1p(-lb) + log_sig
    log_f = jnp.maximum(a, c) + jnp.log(1.0 + jnp.exp(-jnp.abs(a - c)))
    k = 1.0 - jnp.exp(log_f)

    row = lax.broadcasted_iota(jnp.int32, (rows, HEAD_DIM), 0)
    fwd = {HG_CHUNK: _seg_cumsum(log_f, HG_CHUNK, row)}
    rev = {HG_CHUNK: _seg_rev_cumsum_excl(log_f, HG_CHUNK, row)}
    half = HG_CHUNK // 2
    while half >= 1:
        if half >= SUBLANES:
            fwd[half] = _rebase_halves(fwd[2 * half], half, second=True)
            rev[half] = _rebase_halves(rev[2 * half], half, second=False)
        else:
            fwd[half] = _seg_cumsum(log_f, half, row)
            rev[half] = _seg_rev_cumsum_excl(log_f, half, row)
        half //= 2
    b = fwd[HG_CHUNK]
    q_in = (q * jnp.exp(b)).astype(BF16)
    k_out = (k * jnp.exp(rev[HG_CHUNK])).astype(BF16)
    v16 = v.astype(BF16)
    diag = jnp.sum(q * k, axis=1, keepdims=True)

    levels = []
    half = 1
    while half < HG_CHUNK:
        second = (row & (2 * half - 1)) >= half
        ql = jnp.where(second, q * jnp.exp(fwd[half]), 0.0).astype(BF16)
        kl = jnp.where(second, 0.0, k * jnp.exp(rev[half])).astype(BF16)
        levels.append((2 * half, ql, kl))
        half *= 2

    ct = lax.broadcasted_iota(jnp.int32, (HG_CHUNK, HG_CHUNK), 0)
    cs = lax.broadcasted_iota(jnp.int32, (HG_CHUNK, HG_CHUNK), 1)
    st = st_ref[...]
    outs = []
    for ci in range(rows // HG_CHUNK):
        sl = slice(ci * HG_CHUNK, (ci + 1) * HG_CHUNK)
        attn = jnp.zeros((HG_CHUNK, HG_CHUNK), F32)
        for w, ql, kl in levels:
            part = _dot_nt(ql[sl], kl[sl])
            if w < HG_CHUNK:
                part = jnp.where((ct & -w) == (cs & -w), part, 0.0)
            attn = attn + part
        o = _dot(attn.astype(BF16), v16[sl]) + diag[sl] * v[sl] + _dot_nt(q_in[sl], st.astype(BF16))
        outs.append(o)
        b_last = b[(ci + 1) * HG_CHUNK - 1:(ci + 1) * HG_CHUNK, :]
        st = st * jnp.exp(b_last) + _dot_tn(v16[sl], k_out[sl])
    st_ref[...] = st

    o = jnp.concatenate(outs, axis=0)
    g = g_ref[...]
    o_ref[...] = (_rms(o, HEAD_DIM) * gn_ref[...] * (g * _sigmoid(g))).astype(o_ref.dtype)


def hgrn2(proj, lower_bounds, out_norm, layer, batch, seq, rows=1024):
    depth = lower_bounds.shape[0]
    nr = seq // rows
    kern = functools.partial(_hgrn2_kernel, rows=rows, layer=layer)

    def col(c):
        return pl.BlockSpec((rows, HEAD_DIM), lambda b, h, r: (b * nr + r, c * N_HEADS + h))

    return pl.pallas_call(
        kern,
        grid=(batch, N_HEADS, nr),
        in_specs=[pl.BlockSpec((depth, HEAD_DIM), lambda b, h, r: (0, h)),
                  col(0), col(1), col(2), col(3),
                  pl.BlockSpec((1, HEAD_DIM), lambda b, h, r: (0, 0))],
        out_specs=pl.BlockSpec((rows, HEAD_DIM), lambda b, h, r: (b * nr + r, h)),
        out_shape=jax.ShapeDtypeStruct((batch * seq, N_HEADS * HEAD_DIM), BF16),
        scratch_shapes=[pltpu.VMEM((HEAD_DIM, HEAD_DIM), F32)],
        compiler_params=_params("parallel", "parallel", "arbitrary"),
        name="hgrn2",
    )(lower_bounds, proj, proj, proj, proj, out_norm.reshape(1, HEAD_DIM))


def _merge_kernel(a0, a1, a2, w0, w1, w2, g0, g1, g2, o_ref, wbf_ref):
    @pl.when(pl.program_id(1) == 0)
    def _():
        for n, w_ref in enumerate((w0, w1, w2)):
            _stage_weight(wbf_ref.at[n], w_ref)

    acc = _sigmoid(g0[...].astype(F32)) * _dot(a0[...], wbf_ref[0])
    acc = acc + _sigmoid(g1[...].astype(F32)) * _dot(a1[...], wbf_ref[1])
    acc = acc + _sigmoid(g2[...].astype(F32)) * _dot(a2[...], wbf_ref[2])
    o_ref[...] = acc.astype(o_ref.dtype)


def gated_merge(o_sb, o_mla, o_hg, w_sb, w_mla, w_hg, layer, gate_logits, tm=1024, tn=512):
    m, k = o_sb.shape
    n = w_sb.shape[2]
    nb = n // tn
    a_spec = pl.BlockSpec((tm, k), lambda j, i: (i, 0))
    w_spec = pl.BlockSpec((None, k, tn), lambda j, i: (layer, 0, j))

    def g_spec(branch):
        return pl.BlockSpec((tm, tn), lambda j, i: (i, branch * nb + j))

    return pl.pallas_call(
        _merge_kernel,
        grid=(nb, m // tm),
        in_specs=[a_spec, a_spec, a_spec, w_spec, w_spec, w_spec, g_spec(0), g_spec(1), g_spec(2)],
        out_specs=pl.BlockSpec((tm, tn), lambda j, i: (i, j)),
        out_shape=jax.ShapeDtypeStruct((m, n), BF16),
        scratch_shapes=[pltpu.VMEM((3, k, tn), BF16)],
        compiler_params=_params("arbitrary", "arbitrary"),
        name="gated_merge",
    )(o_sb, o_mla, o_hg, w_sb, w_mla, w_hg, gate_logits, gate_logits, gate_logits)


HALO = 16

def _up_conv_glu_kernel(a_ref, ah_ref, wg_ref, wv_ref, cg_ref, cv_ref, o_ref, wbf_ref, *, tiles_per_seq):
    @pl.when(pl.program_id(1) == 0)
    def _():
        _stage_weight(wbf_ref.at[0], wg_ref)
        _stage_weight(wbf_ref.at[1], wv_ref)

    a = a_ref[...]
    ah = ah_ref[...]
    seq_start = (pl.program_id(1) % tiles_per_seq) == 0
    row = lax.broadcasted_iota(jnp.int32, o_ref.shape, 0)

    def conv(slot, c_ref):
        w = wbf_ref[slot]
        u = _dot(a, w)
        halo = jnp.where(seq_start, 0.0, _dot(ah, w))
        prev1 = halo[HALO - 1:HALO]
        prev2 = halo[HALO - 2:HALO - 1]
        u1 = jnp.where(row == 0, prev1, pltpu.roll(u, 1, 0))
        u2 = jnp.where(row == 0, prev2, jnp.where(row == 1, prev1, pltpu.roll(u, 2, 0)))
        c = c_ref[...]
        return u2 * c[0:1] + u1 * c[1:2] + u * c[2:3]

    gate = conv(0, cg_ref)
    val = conv(1, cv_ref)
    o_ref[...] = (gate * _sigmoid(gate) * val).astype(o_ref.dtype)


def up_conv_glu(h, w_up, conv_w, layer, seq, tm=1024, tn=256):
    m, k = h.shape
    d_ff = w_up.shape[2] // 2
    nb = d_ff // tn
    kern = functools.partial(_up_conv_glu_kernel, tiles_per_seq=seq // tm)
    halo_blocks = tm // HALO
    return pl.pallas_call(
        kern,
        grid=(nb, m // tm),
        in_specs=[pl.BlockSpec((tm, k), lambda j, i: (i, 0)),
                  pl.BlockSpec((HALO, k), lambda j, i: (jnp.maximum(i * halo_blocks - 1, 0), 0)),
                  pl.BlockSpec((None, k, tn), lambda j, i: (layer, 0, j)),
                  pl.BlockSpec((None, k, tn), lambda j, i: (layer, 0, nb + j)),
                  pl.BlockSpec((None, conv_w.shape[1], tn), lambda j, i: (layer, 0, j)),
                  pl.BlockSpec((None, conv_w.shape[1], tn), lambda j, i: (layer, 0, nb + j))],
        out_specs=pl.BlockSpec((tm, tn), lambda j, i: (i, j)),
        out_shape=jax.ShapeDtypeStruct((m, d_ff), BF16),
        scratch_shapes=[pltpu.VMEM((2, k, tn), BF16)],
        compiler_params=_params("arbitrary", "arbitrary"),
        name="up_conv_glu",
    )(h, h, w_up, w_up, conv_w, conv_w)


def _rope_lane_tables(seq):
    pos = jnp.arange(seq, dtype=F32)
    inv = ROPE_THETA ** (-jnp.arange(0, MLA_ROPE, 2, dtype=F32) / MLA_ROPE)
    ang = pos[:, None] * inv[None, :]
    cos, sin = jnp.cos(ang), jnp.sin(ang)
    zeros = jnp.zeros((seq, HEAD_DIM - MLA_ROPE), F32)
    return (jnp.concatenate([cos, cos, zeros], axis=1),
            jnp.concatenate([-sin, sin, zeros], axis=1))


def _pad_head_gain(g):
    return jnp.pad(g, (0, MLA_PAD - MLA_QK)).reshape(1, MLA_PAD)


def kernel(x, lower_bounds, attn_norm, w_in, sb_q_norm, sb_k_norm, mla_q_a_norm, mla_kv_a_norm, mla_w_q_b, mla_w_kv_b, mla_q_norm, mla_k_norm, hg_out_norm, w_branch_sb, w_branch_mla, w_branch_hg, w_out, ffn_norm, w_up, ffn_conv, w_down):
    batch, seq, d_model = x.shape
    depth = w_in.shape[0]
    width = N_HEADS * HEAD_DIM
    q_lora = mla_w_q_b.shape[1]
    kv_lora = mla_w_kv_b.shape[1]
    c_attn = 3 * width + q_lora + kv_lora
    c_hg = c_attn + MLA_ROPE
    c_gate = c_hg + 4 * width

    cos_t, sin_t = _rope_lane_tables(seq)
    xf = x.reshape(batch * seq, d_model)
    w_in_t = jnp.swapaxes(w_in, 1, 2)
    for l in range(depth):
        wq = jnp.pad(mla_w_q_b[l], ((0, 0), (0, 0), (0, MLA_PAD - MLA_QK))).reshape(q_lora, N_HEADS * MLA_PAD)
        wkn = mla_w_kv_b[l][:, :, :HEAD_DIM].reshape(kv_lora, width)
        wv = mla_w_kv_b[l][:, :, HEAD_DIM:].reshape(kv_lora, width)

        h = rmsnorm_rows(xf, attn_norm[l])
        p_attn = weight_matmul(h, w_in_t, l, BF16, 1024, 512, transposed=True, n=c_attn)
        p_kr = weight_matmul(h, w_in_t, l, F32, 1024, HEAD_DIM, transposed=True, col0=c_attn, n=HEAD_DIM)
        p_hg = weight_matmul(h, w_in_t, l, F32, 1024, 512, transposed=True, col0=c_hg, n=c_gate - c_hg)
        p_gate = weight_matmul(h, w_in_t, l, BF16, 1024, 512, transposed=True, col0=c_gate)

        sb_qk = headnorm_qk(p_attn, jnp.stack([sb_q_norm[l], sb_k_norm[l]]).reshape(2, 1, HEAD_DIM), width)
        o_sb = sb_attention(sb_qk, p_attn, batch, seq)

        mq, mk, mv = mla_prep(p_attn, p_kr, mla_q_a_norm[l].reshape(1, q_lora),
                              mla_kv_a_norm[l].reshape(1, kv_lora), wq.astype(BF16), wkn.astype(BF16),
                              wv.astype(BF16), _pad_head_gain(mla_q_norm[l]), _pad_head_gain(mla_k_norm[l]),
                              cos_t, sin_t, seq, q_lora, kv_lora)
        o_mla = mla_attention(mq, mk, mv, batch, seq)

        o_hg = hgrn2(p_hg, lower_bounds, hg_out_norm[l], l, batch, seq)

        merged = gated_merge(o_sb, o_mla, o_hg, w_branch_sb, w_branch_mla, w_branch_hg, l, p_gate)
        xf = weight_matmul(merged, w_out, l, F32, 1024, 512, res=xf)

        h2 = rmsnorm_rows(xf, ffn_norm[l])
        act = up_conv_glu(h2, w_up, ffn_conv, l, seq)
        for part in range(2):
            xf = weight_matmul(act, w_down, l, F32, 1024, 512, k_part=(part, 2), res=xf)
    return xf.reshape(batch, seq, d_model)
```

```python
import functools

import jax
import jax.numpy as jnp
from jax import lax
from jax.experimental import pallas as pl
from jax.experimental.pallas import tpu as pltpu

F32 = jnp.float32
BF16 = jnp.bfloat16

SUBLANES = 8
HEAD_DIM = 128
N_HEADS = 8
MLA_ROPE = 64
MLA_QK = HEAD_DIM + MLA_ROPE
MLA_PAD = 256
HG_CHUNK = 64
ROPE_THETA = 10000.0
EPS = 1e-6
MASK_NEG = -1e30
LB_FLOOR = 1e-30

VMEM_LIMIT_BYTES = 56 * 1024 * 1024


def _params(*sem):
    return pltpu.CompilerParams(dimension_semantics=sem, vmem_limit_bytes=VMEM_LIMIT_BYTES)


def _dot(a, b):
    return jnp.dot(a, b, preferred_element_type=F32)


def _dot_nt(a, b):
    return lax.dot_general(a, b, (((1,), (1,)), ((), ())), preferred_element_type=F32)


def _dot_tn(a, b):
    return lax.dot_general(a, b, (((0,), (0,)), ((), ())), preferred_element_type=F32)


def _rms(x, width):
    return x * lax.rsqrt(jnp.sum(x * x, axis=-1, keepdims=True) / width + EPS)


def _sigmoid(x):
    return 0.5 * jnp.tanh(0.5 * x) + 0.5


def _rmsnorm_kernel(x_ref, g_ref, o_ref):
    x = x_ref[...]
    o_ref[...] = (_rms(x, x.shape[-1]) * g_ref[...]).astype(o_ref.dtype)


def rmsnorm_rows(x, g, tm=256):
    m, d = x.shape
    return pl.pallas_call(
        _rmsnorm_kernel,
        grid=(m // tm,),
        in_specs=[pl.BlockSpec((tm, d), lambda i: (i, 0)),
                  pl.BlockSpec((1, d), lambda i: (0, 0))],
        out_specs=pl.BlockSpec((tm, d), lambda i: (i, 0)),
        out_shape=jax.ShapeDtypeStruct((m, d), BF16),
        compiler_params=_params("parallel"),
        name="rmsnorm_rows",
    )(x, g.reshape(1, d))


CAST_ELEMS = 512 * 512


def _stage_weight(dst_ref, w_ref):
    nrows, ncols = dst_ref.shape
    rows = min(nrows, CAST_ELEMS // ncols)

    def body(r, carry):
        sl = pl.ds(pl.multiple_of(r * rows, rows), rows)
        dst_ref[sl, :] = w_ref[sl, :].astype(BF16)
        return carry

    lax.fori_loop(0, nrows // rows, body, 0)


def _wmm_kernel(*refs, transposed, residual):
    a_ref, w_ref = refs[0], refs[1]
    r_ref = refs[2] if residual else None
    o_ref, wbf_ref = refs[-2], refs[-1]

    @pl.when(pl.program_id(1) == 0)
    def _():
        _stage_weight(wbf_ref, w_ref.at[0] if transposed else w_ref)

    d = (_dot_nt if transposed else _dot)(a_ref[...], wbf_ref[...])
    if residual:
        d = r_ref[...] + d
    o_ref[...] = d.astype(o_ref.dtype)


def weight_matmul(a, w, layer, out_dtype, tm, tn, *, transposed=False, col0=0, n=None, k_part=(0, 1), res=None):
    m = a.shape[0]
    kp, parts = k_part
    k = a.shape[1] // parts
    if transposed:
        assert parts == 1 and w.shape[2] == k
        n = w.shape[1] - col0 if n is None else n
        w_spec = pl.BlockSpec((pl.Element(1), pl.Element(tn), pl.Element(k)),
                              lambda j, i: (layer, pl.multiple_of(col0 + j * tn, SUBLANES), 0))
        staged = (tn, k)
    else:
        assert col0 % tn == 0 and w.shape[1] == k * parts
        n = w.shape[2] - col0 if n is None else n
        w_spec = pl.BlockSpec((None, k, tn), lambda j, i: (layer, kp, col0 // tn + j))
        staged = (k, tn)
    assert m % tm == 0 and n % tn == 0 and a.shape[1] == k * parts
    in_specs = [pl.BlockSpec((tm, k), lambda j, i: (i, kp)), w_spec]
    args = [a, w]
    if res is not None:
        in_specs.append(pl.BlockSpec((tm, tn), lambda j, i: (i, j)))
        args.append(res)
    return pl.pallas_call(
        functools.partial(_wmm_kernel, transposed=transposed, residual=res is not None),
        grid=(n // tn, m // tm),
        in_specs=in_specs,
        out_specs=pl.BlockSpec((tm, tn), lambda j, i: (i, j)),
        out_shape=jax.ShapeDtypeStruct((m, n), out_dtype),
        scratch_shapes=[pltpu.VMEM(staged, BF16)],
        compiler_params=_params("arbitrary", "arbitrary"),
        name="weight_matmul",
    )(*args)


def _headnorm_kernel(x_ref, g_ref, o_ref):
    g = g_ref[0]
    for h in range(x_ref.shape[1] // HEAD_DIM):
        sl = slice(h * HEAD_DIM, (h + 1) * HEAD_DIM)
        x = x_ref[:, sl].astype(F32)
        o_ref[:, sl] = (_rms(x, HEAD_DIM) * g).astype(o_ref.dtype)


def headnorm_qk(proj, gains, width, tm=512):
    m = proj.shape[0]
    return pl.pallas_call(
        _headnorm_kernel,
        grid=(m // tm, 2),
        in_specs=[pl.BlockSpec((tm, width), lambda i, j: (i, j)),
                  pl.BlockSpec((1, 1, HEAD_DIM), lambda i, j: (j, 0, 0))],
        out_specs=pl.BlockSpec((tm, width), lambda i, j: (i, j)),
        out_shape=jax.ShapeDtypeStruct((m, 2 * width), BF16),
        compiler_params=_params("parallel", "arbitrary"),
        name="headnorm_qk",
    )(proj, gains)


def _sb_block(q, ks, vs, carry, acc, from_here, scale, strict):
    z = _dot_nt(q, ks) * scale
    log_1m_beta = -(jnp.maximum(z, 0.0) + jnp.log(1.0 + jnp.exp(-jnp.abs(z))))
    if strict is not None:
        log_1m_beta = jnp.where(strict, log_1m_beta, 0.0)
    hi = log_1m_beta.astype(BF16)
    lo = (log_1m_beta - hi.astype(F32)).astype(BF16)
    incl = _dot(jnp.concatenate([hi, lo], axis=1), from_here)
    w = jnp.exp(z + incl + carry)
    if strict is not None:
        w = jnp.where(strict, w, 0.0)
    acc = acc + _dot(w.astype(BF16), vs)
    carry = carry + jnp.sum(log_1m_beta, axis=1, keepdims=True)
    return carry, acc


def _sb_attn_kernel(q_ref, k_ref, v_ref, o_ref, *, rows, kb, scale):
    qi = pl.program_id(2)
    per = rows // kb
    row = lax.broadcasted_iota(jnp.int32, (rows, kb), 0)
    col = lax.broadcasted_iota(jnp.int32, (rows, kb), 1)
    later = lax.broadcasted_iota(jnp.int32, (kb, kb), 0) >= lax.broadcasted_iota(jnp.int32, (kb, kb), 1)
    tri = later.astype(BF16)
    block = functools.partial(_sb_block, from_here=jnp.concatenate([tri, tri], axis=0), scale=scale)
    halves = (q_ref[:rows, :], q_ref[rows:, :])

    def kv(j):
        start = pl.multiple_of(j * kb, kb)
        return k_ref[pl.ds(start, kb), :], v_ref[pl.ds(start, kb), :]

    first_block = qi * 2 * per
    state = [(jnp.zeros((rows, 1), F32), jnp.zeros((rows, HEAD_DIM), F32)) for _ in halves]
    for d in reversed(range(2 * per)):
        ks, vs = kv(first_block + d)
        for c, q in enumerate(halves):
            start = c * rows
            if d * kb >= start + rows:
                continue
            strict = None if (d + 1) * kb <= start else (col + d * kb) < (row + start)
            state[c] = block(q, ks, vs, *state[c], strict=strict)

    def body(t, st):
        sa, sb = st
        for u in range(2):
            ks, vs = kv(first_block - 1 - 2 * t - u)
            sa = block(halves[0], ks, vs, *sa, strict=None)
            sb = block(halves[1], ks, vs, *sb, strict=None)
        return sa, sb

    sa, sb = lax.fori_loop(0, qi * per, body, tuple(state))
    o_ref[:rows, :] = sa[1].astype(o_ref.dtype)
    o_ref[rows:, :] = sb[1].astype(o_ref.dtype)


def sb_attention(qk, proj, batch, seq, rows=1024, kb=256):
    width = N_HEADS * HEAD_DIM
    assert seq % (2 * rows) == 0 and rows % kb == 0
    nq = seq // (2 * rows)
    kern = functools.partial(_sb_attn_kernel, rows=rows, kb=kb, scale=HEAD_DIM ** -0.5)
    return pl.pallas_call(
        kern,
        grid=(batch, N_HEADS, nq),
        in_specs=[pl.BlockSpec((2 * rows, HEAD_DIM), lambda b, h, i: (b * nq + i, h)),
                  pl.BlockSpec((seq, HEAD_DIM), lambda b, h, i: (b, N_HEADS + h)),
                  pl.BlockSpec((seq, HEAD_DIM), lambda b, h, i: (b, 2 * N_HEADS + h))],
        out_specs=pl.BlockSpec((2 * rows, HEAD_DIM), lambda b, h, i: (b * nq + i, h)),
        out_shape=jax.ShapeDtypeStruct((batch * seq, width), BF16),
        compiler_params=_params("parallel", "parallel", "arbitrary"),
        name="sb_attention",
    )(qk, qk, proj)


def _rope_lanes(x, cos_t, sin_t):
    lane = lax.broadcasted_iota(jnp.int32, x.shape, 1)
    half = MLA_ROPE // 2
    partner = jnp.where(lane < half, pltpu.roll(x, HEAD_DIM - half, 1), pltpu.roll(x, half, 1))
    return x * cos_t + partner * sin_t


def _mla_prep_kernel(p_ref, kr_ref, qa_ref, kva_ref, wq_ref, wkn_ref, wv_ref, gq_ref, gk_ref,
                     cos_ref, sin_ref, q_out, k_out, v_out, *, q_lora):
    cos_t = cos_ref[...]
    sin_t = sin_ref[...]
    gq = gq_ref[...]
    gk = gk_ref[...]
    cq = p_ref[:, :q_lora].astype(F32)
    cqn = (_rms(cq, q_lora) * qa_ref[...]).astype(BF16)
    qraw = _dot(cqn, wq_ref[...])
    for h in range(N_HEADS):
        blk = qraw[:, h * MLA_PAD:(h + 1) * MLA_PAD]
        y = _rms(blk, MLA_QK) * gq
        q_out[:, h * MLA_PAD:h * MLA_PAD + HEAD_DIM] = y[:, :HEAD_DIM].astype(q_out.dtype)
        q_out[:, h * MLA_PAD + HEAD_DIM:(h + 1) * MLA_PAD] = _rope_lanes(
            y[:, HEAD_DIM:], cos_t, sin_t).astype(q_out.dtype)

    ckv = p_ref[:, q_lora:].astype(F32)
    ckvn = (_rms(ckv, ckv.shape[-1]) * kva_ref[...]).astype(BF16)
    kn = _dot(ckvn, wkn_ref[...])
    v_out[...] = _dot(ckvn, wv_ref[...]).astype(v_out.dtype)
    kr = kr_ref[...]
    lane = lax.broadcasted_iota(jnp.int32, kr.shape, 1)
    kr = jnp.where(lane < MLA_ROPE, kr, 0.0)
    kr_ss = jnp.sum(kr * kr, axis=-1, keepdims=True)
    for h in range(N_HEADS):
        kb = kn[:, h * HEAD_DIM:(h + 1) * HEAD_DIM]
        inv = lax.rsqrt((jnp.sum(kb * kb, axis=-1, keepdims=True) + kr_ss) / MLA_QK + EPS)
        k_out[:, h * MLA_PAD:h * MLA_PAD + HEAD_DIM] = (kb * inv * gk[:, :HEAD_DIM]).astype(k_out.dtype)
        k_out[:, h * MLA_PAD + HEAD_DIM:(h + 1) * MLA_PAD] = _rope_lanes(
            kr * inv * gk[:, HEAD_DIM:], cos_t, sin_t).astype(k_out.dtype)


def mla_prep(proj, kr, qa, kva, wq, wkn, wv, gq, gk, cos_t, sin_t, seq, q_lora, kv_lora, tm=512):
    m = proj.shape[0]
    lat = q_lora + kv_lora
    lat_blk = (3 * N_HEADS * HEAD_DIM) // lat
    assert lat_blk * lat == 3 * N_HEADS * HEAD_DIM
    ns = seq // tm
    full = lambda i: (0, 0)
    kern = functools.partial(_mla_prep_kernel, q_lora=q_lora)
    return pl.pallas_call(
        kern,
        grid=(m // tm,),
        in_specs=[pl.BlockSpec((tm, lat), lambda i: (i, lat_blk)),
                  pl.BlockSpec((tm, HEAD_DIM), lambda i: (i, 0)),
                  pl.BlockSpec((1, q_lora), full),
                  pl.BlockSpec((1, kv_lora), full),
                  pl.BlockSpec(wq.shape, full),
                  pl.BlockSpec(wkn.shape, full),
                  pl.BlockSpec(wv.shape, full),
                  pl.BlockSpec((1, MLA_PAD), full),
                  pl.BlockSpec((1, MLA_PAD), full),
                  pl.BlockSpec((tm, HEAD_DIM), lambda i: (i % ns, 0)),
                  pl.BlockSpec((tm, HEAD_DIM), lambda i: (i % ns, 0))],
        out_specs=[pl.BlockSpec((tm, N_HEADS * MLA_PAD), lambda i: (i, 0)),
                   pl.BlockSpec((tm, N_HEADS * MLA_PAD), lambda i: (i, 0)),
                   pl.BlockSpec((tm, N_HEADS * HEAD_DIM), lambda i: (i, 0))],
        out_shape=[jax.ShapeDtypeStruct((m, N_HEADS * MLA_PAD), BF16),
                   jax.ShapeDtypeStruct((m, N_HEADS * MLA_PAD), BF16),
                   jax.ShapeDtypeStruct((m, N_HEADS * HEAD_DIM), BF16)],
        compiler_params=_params("parallel"),
        name="mla_prep",
    )(proj, kr, qa, kva, wq, wkn, wv, gq, gk, cos_t, sin_t)


def _softmax_block(q, ks, vs, state, scale, causal):
    m, l, acc = state
    z = _dot_nt(q, ks)
    if causal is not None:
        z = jnp.where(causal, z, MASK_NEG)
    m_new = jnp.maximum(m, jnp.max(z, axis=1, keepdims=True))
    p = jnp.exp((z - m_new) * scale)
    alpha = jnp.exp((m - m_new) * scale)
    l = alpha * l + jnp.sum(p, axis=1, keepdims=True)
    acc = alpha * acc + _dot(p.astype(BF16), vs)
    return m_new, l, acc


def _mla_attn_kernel(q_ref, k_ref, v_ref, o_ref, *, blk, scale):
    qi = pl.program_id(2)
    qa = q_ref[:blk, :]
    qb = q_ref[blk:, :]
    row = lax.broadcasted_iota(jnp.int32, (blk, 2 * blk), 0)
    col = lax.broadcasted_iota(jnp.int32, (blk, 2 * blk), 1)
    block = functools.partial(_softmax_block, scale=scale)
    init = (jnp.full((blk, 1), MASK_NEG, F32), jnp.zeros((blk, 1), F32), jnp.zeros((blk, HEAD_DIM), F32))

    def kv(j):
        start = pl.multiple_of(j * 2 * blk, 2 * blk)
        return k_ref[pl.ds(start, 2 * blk), :], v_ref[pl.ds(start, 2 * blk), :]

    def body(t, c):
        sa, sb = c
        ks, vs = kv(t)
        return block(qa, ks, vs, sa, causal=None), block(qb, ks, vs, sb, causal=None)

    sa, sb = lax.fori_loop(0, qi, body, (init, init))
    ks, vs = kv(qi)
    sa = block(qa, ks[:blk], vs[:blk], sa, causal=(col <= row)[:, :blk])
    sb = block(qb, ks, vs, sb, causal=col <= row + blk)
    o_ref[:blk, :] = (sa[2] / sa[1]).astype(o_ref.dtype)
    o_ref[blk:, :] = (sb[2] / sb[1]).astype(o_ref.dtype)


def mla_attention(q, k, v, batch, seq, blk=1024):
    assert seq % (2 * blk) == 0
    nq = seq // (2 * blk)
    kern = functools.partial(_mla_attn_kernel, blk=blk, scale=MLA_QK ** -0.5)
    return pl.pallas_call(
        kern,
        grid=(batch, N_HEADS, nq),
        in_specs=[pl.BlockSpec((2 * blk, MLA_PAD), lambda b, h, i: (b * nq + i, h)),
                  pl.BlockSpec((seq, MLA_PAD), lambda b, h, i: (b, h)),
                  pl.BlockSpec((seq, HEAD_DIM), lambda b, h, i: (b, h))],
        out_specs=pl.BlockSpec((2 * blk, HEAD_DIM), lambda b, h, i: (b * nq + i, h)),
        out_shape=jax.ShapeDtypeStruct((batch * seq, N_HEADS * HEAD_DIM), BF16),
        compiler_params=_params("parallel", "parallel", "arbitrary"),
        name="mla_attention",
    )(q, k, v)


def _seg_cumsum(x, seg, row):
    pos = row & (seg - 1)
    s = 1
    while s < seg:
        x = x + jnp.where(pos >= s, pltpu.roll(x, s, 0), 0.0)
        s *= 2
    return x


def _seg_rev_cumsum_excl(x, seg, row):
    pos = row & (seg - 1)
    n = x.shape[0]
    y = x
    s = 1
    while s < seg:
        y = y + jnp.where(pos < seg - s, pltpu.roll(y, n - s, 0), 0.0)
        s *= 2
    return y - x


def _rebase_halves(x, half, second):
    pieces = []
    for s in range(0, x.shape[0], 2 * half):
        pivot = x[s + half - 1:s + half]
        lo, hi = x[s:s + half], x[s + half:s + 2 * half]
        pieces += [lo, hi - pivot] if second else [lo - pivot, hi]
    return jnp.concatenate(pieces, axis=0)


def _hgrn2_kernel(lb_ref, q_ref, z_ref, i_ref, g_ref, gn_ref, o_ref, st_ref, *, rows, layer):
    @pl.when(pl.program_id(2) == 0)
    def _():
        st_ref[...] = jnp.zeros_like(st_ref)

    lbp = lb_ref[...]
    e = jnp.exp(lbp - jnp.max(lbp, axis=0, keepdims=True))
    p = e / jnp.sum(e, axis=0, keepdims=True)
    lb = jnp.sum(p[:layer + 1], axis=0, keepdims=True) - p[0:1]

    z = z_ref[...]
    q = q_ref[...]
    v = i_ref[...]
    log_sig = jnp.minimum(z, 0.0) - jnp.log(1.0 + jnp.exp(-jnp.abs(z)))
    a = jnp.log(jnp.maximum(lb, LB_FLOOR))
    c = jnp.log1p(-lb) + log_sig
    log_f = jnp.maximum(a, c) + jnp.log(1.0 + jnp.exp(-jnp.abs(a - c)))
    k = 1.0 - jnp.exp(log_f)

    row = lax.broadcasted_iota(jnp.int32, (rows, HEAD_DIM), 0)
    fwd = {HG_CHUNK: _seg_cumsum(log_f, HG_CHUNK, row)}
    rev = {HG_CHUNK: _seg_rev_cumsum_excl(log_f, HG_CHUNK, row)}
    half = HG_CHUNK // 2
    while half >= 1:
        if half >= SUBLANES:
            fwd[half] = _rebase_halves(fwd[2 * half], half, second=True)
            rev[half] = _rebase_halves(rev[2 * half], half, second=False)
        else:
            fwd[half] = _seg_cumsum(log_f, half, row)
            rev[half] = _seg_rev_cumsum_excl(log_f, half, row)
        half //= 2
    b = fwd[HG_CHUNK]
    q_in = (q * jnp.exp(b)).astype(BF16)
    k_out = (k * jnp.exp(rev[HG_CHUNK])).astype(BF16)
    v16 = v.astype(BF16)
    diag = jnp.sum(q * k, axis=1, keepdims=True)

    levels = []
    half = 1
    while half < HG_CHUNK:
        second = (row & (2 * half - 1)) >= half
        ql = jnp.where(second, q * jnp.exp(fwd[half]), 0.0).astype(BF16)
        kl = jnp.where(second, 0.0, k * jnp.exp(rev[half])).astype(BF16)
        levels.append((2 * half, ql, kl))
        half *= 2

    ct = lax.broadcasted_iota(jnp.int32, (HG_CHUNK, HG_CHUNK), 0)
    cs = lax.broadcasted_iota(jnp.int32, (HG_CHUNK, HG_CHUNK), 1)
    st = st_ref[...]
    outs = []
    for ci in range(rows // HG_CHUNK):
        sl = slice(ci * HG_CHUNK, (ci + 1) * HG_CHUNK)
        attn = jnp.zeros((HG_CHUNK, HG_CHUNK), F32)
        for w, ql, kl in levels:
            part = _dot_nt(ql[sl], kl[sl])
            if w < HG_CHUNK:
                part = jnp.where((ct & -w) == (cs & -w), part, 0.0)
            attn = attn + part
        o = _dot(attn.astype(BF16), v16[sl]) + diag[sl] * v[sl] + _dot_nt(q_in[sl], st.astype(BF16))
        outs.append(o)
        b_last = b[(ci + 1) * HG_CHUNK - 1:(ci + 1) * HG_CHUNK, :]
        st = st * jnp.exp(b_last) + _dot_tn(v16[sl], k_out[sl])
    st_ref[...] = st

    o = jnp.concatenate(outs, axis=0)
    g = g_ref[...]
    o_ref[...] = (_rms(o, HEAD_DIM) * gn_ref[...] * (g * _sigmoid(g))).astype(o_ref.dtype)


def hgrn2(proj, lower_bounds, out_norm, layer, batch, seq, rows=2048):
    depth = lower_bounds.shape[0]
    assert seq % rows == 0 and rows % HG_CHUNK == 0
    nr = seq // rows
    kern = functools.partial(_hgrn2_kernel, rows=rows, layer=layer)

    def col(c):
        return pl.BlockSpec((rows, HEAD_DIM), lambda b, h, r: (b * nr + r, c * N_HEADS + h))

    return pl.pallas_call(
        kern,
        grid=(batch, N_HEADS, nr),
        in_specs=[pl.BlockSpec((depth, HEAD_DIM), lambda b, h, r: (0, h)),
                  col(0), col(1), col(2), col(3),
                  pl.BlockSpec((1, HEAD_DIM), lambda b, h, r: (0, 0))],
        out_specs=pl.BlockSpec((rows, HEAD_DIM), lambda b, h, r: (b * nr + r, h)),
        out_shape=jax.ShapeDtypeStruct((batch * seq, N_HEADS * HEAD_DIM), BF16),
        scratch_shapes=[pltpu.VMEM((HEAD_DIM, HEAD_DIM), F32)],
        compiler_params=_params("parallel", "parallel", "arbitrary"),
        name="hgrn2",
    )(lower_bounds, proj, proj, proj, proj, out_norm.reshape(1, HEAD_DIM))


def _merge_kernel(a0, a1, a2, w0, w1, w2, g0, g1, g2, o_ref, wbf_ref):
    @pl.when(pl.program_id(1) == 0)
    def _():
        for n, w_ref in enumerate((w0, w1, w2)):
            _stage_weight(wbf_ref.at[n], w_ref)

    acc = _sigmoid(g0[...].astype(F32)) * _dot(a0[...], wbf_ref[0])
    acc = acc + _sigmoid(g1[...].astype(F32)) * _dot(a1[...], wbf_ref[1])
    acc = acc + _sigmoid(g2[...].astype(F32)) * _dot(a2[...], wbf_ref[2])
    o_ref[...] = acc.astype(o_ref.dtype)


def gated_merge(o_sb, o_mla, o_hg, w_sb, w_mla, w_hg, layer, gate_logits, tm=1024, tn=512):
    m, k = o_sb.shape
    n = w_sb.shape[2]
    nb = n // tn
    a_spec = pl.BlockSpec((tm, k), lambda j, i: (i, 0))
    w_spec = pl.BlockSpec((None, k, tn), lambda j, i: (layer, 0, j))

    def g_spec(branch):
        return pl.BlockSpec((tm, tn), lambda j, i: (i, branch * nb + j))

    return pl.pallas_call(
        _merge_kernel,
        grid=(nb, m // tm),
        in_specs=[a_spec, a_spec, a_spec, w_spec, w_spec, w_spec, g_spec(0), g_spec(1), g_spec(2)],
        out_specs=pl.BlockSpec((tm, tn), lambda j, i: (i, j)),
        out_shape=jax.ShapeDtypeStruct((m, n), BF16),
        scratch_shapes=[pltpu.VMEM((3, k, tn), BF16)],
        compiler_params=_params("arbitrary", "arbitrary"),
        name="gated_merge",
    )(o_sb, o_mla, o_hg, w_sb, w_mla, w_hg, gate_logits, gate_logits, gate_logits)


HALO = 16

def _up_conv_glu_kernel(a_ref, ah_ref, wg_ref, wv_ref, cg_ref, cv_ref, o_ref, wbf_ref, *, tiles_per_seq):
    @pl.when(pl.program_id(1) == 0)
    def _():
        _stage_weight(wbf_ref.at[0], wg_ref)
        _stage_weight(wbf_ref.at[1], wv_ref)

    a = a_ref[...]
    ah = ah_ref[...]
    seq_start = (pl.program_id(1) % tiles_per_seq) == 0
    row = lax.broadcasted_iota(jnp.int32, o_ref.shape, 0)

    def conv(slot, c_ref):
        w = wbf_ref[slot]
        u = _dot(a, w)
        halo = jnp.where(seq_start, 0.0, _dot(ah, w))
        prev1 = halo[HALO - 1:HALO]
        prev2 = halo[HALO - 2:HALO - 1]
        u1 = jnp.where(row == 0, prev1, pltpu.roll(u, 1, 0))
        u2 = jnp.where(row == 0, prev2, jnp.where(row == 1, prev1, pltpu.roll(u, 2, 0)))
        c = c_ref[...]
        return u2 * c[0:1] + u1 * c[1:2] + u * c[2:3]

    gate = conv(0, cg_ref)
    val = conv(1, cv_ref)
    o_ref[...] = (gate * _sigmoid(gate) * val).astype(o_ref.dtype)


def up_conv_glu(h, w_up, conv_w, layer, seq, tm=1024, tn=256):
    m, k = h.shape
    d_ff = w_up.shape[2] // 2
    nb = d_ff // tn
    kern = functools.partial(_up_conv_glu_kernel, tiles_per_seq=seq // tm)
    halo_blocks = tm // HALO
    return pl.pallas_call(
        kern,
        grid=(nb, m // tm),
        in_specs=[pl.BlockSpec((tm, k), lambda j, i: (i, 0)),
                  pl.BlockSpec((HALO, k), lambda j, i: (jnp.maximum(i * halo_blocks - 1, 0), 0)),
                  pl.BlockSpec((None, k, tn), lambda j, i: (layer, 0, j)),
                  pl.BlockSpec((None, k, tn), lambda j, i: (layer, 0, nb + j)),
                  pl.BlockSpec((None, conv_w.shape[1], tn), lambda j, i: (layer, 0, j)),
                  pl.BlockSpec((None, conv_w.shape[1], tn), lambda j, i: (layer, 0, nb + j))],
        out_specs=pl.BlockSpec((tm, tn), lambda j, i: (i, j)),
        out_shape=jax.ShapeDtypeStruct((m, d_ff), BF16),
        scratch_shapes=[pltpu.VMEM((2, k, tn), BF16)],
        compiler_params=_params("arbitrary", "arbitrary"),
        name="up_conv_glu",
    )(h, h, w_up, w_up, conv_w, conv_w)


def _rope_lane_tables(seq):
    pos = jnp.arange(seq, dtype=F32)
    inv = ROPE_THETA ** (-jnp.arange(0, MLA_ROPE, 2, dtype=F32) / MLA_ROPE)
    ang = pos[:, None] * inv[None, :]
    cos, sin = jnp.cos(ang), jnp.sin(ang)
    zeros = jnp.zeros((seq, HEAD_DIM - MLA_ROPE), F32)
    return (jnp.concatenate([cos, cos, zeros], axis=1),
            jnp.concatenate([-sin, sin, zeros], axis=1))


def _pad_head_gain(g):
    return jnp.pad(g, (0, MLA_PAD - MLA_QK)).reshape(1, MLA_PAD)


def kernel(x, lower_bounds, attn_norm, w_in, sb_q_norm, sb_k_norm, mla_q_a_norm, mla_kv_a_norm, mla_w_q_b, mla_w_kv_b, mla_q_norm, mla_k_norm, hg_out_norm, w_branch_sb, w_branch_mla, w_branch_hg, w_out, ffn_norm, w_up, ffn_conv, w_down):
    batch, seq, d_model = x.shape
    depth = w_in.shape[0]
    width = N_HEADS * HEAD_DIM
    q_lora = mla_w_q_b.shape[1]
    kv_lora = mla_w_kv_b.shape[1]
    c_attn = 3 * width + q_lora + kv_lora
    c_hg = c_attn + MLA_ROPE
    c_gate = c_hg + 4 * width

    cos_t, sin_t = _rope_lane_tables(seq)
    xf = x.reshape(batch * seq, d_model)
    w_in_t = jnp.swapaxes(w_in, 1, 2)
    for l in range(depth):
        wq = jnp.pad(mla_w_q_b[l], ((0, 0), (0, 0), (0, MLA_PAD - MLA_QK))).reshape(q_lora, N_HEADS * MLA_PAD)
        wkn = mla_w_kv_b[l][:, :, :HEAD_DIM].reshape(kv_lora, width)
        wv = mla_w_kv_b[l][:, :, HEAD_DIM:].reshape(kv_lora, width)

        h = rmsnorm_rows(xf, attn_norm[l])
        p_attn = weight_matmul(h, w_in_t, l, BF16, 1024, 512, transposed=True, n=c_attn)
        p_kr = weight_matmul(h, w_in_t, l, F32, 1024, HEAD_DIM, transposed=True, col0=c_attn, n=HEAD_DIM)
        p_hg = weight_matmul(h, w_in_t, l, F32, 1024, 512, transposed=True, col0=c_hg, n=c_gate - c_hg)
        p_gate = weight_matmul(h, w_in_t, l, BF16, 1024, 512, transposed=True, col0=c_gate)

        sb_qk = headnorm_qk(p_attn, jnp.stack([sb_q_norm[l], sb_k_norm[l]]).reshape(2, 1, HEAD_DIM), width)
        o_sb = sb_attention(sb_qk, p_attn, batch, seq)

        mq, mk, mv = mla_prep(p_attn, p_kr, mla_q_a_norm[l].reshape(1, q_lora),
                              mla_kv_a_norm[l].reshape(1, kv_lora), wq.astype(BF16), wkn.astype(BF16),
                              wv.astype(BF16), _pad_head_gain(mla_q_norm[l]), _pad_head_gain(mla_k_norm[l]),
                              cos_t, sin_t, seq, q_lora, kv_lora)
        o_mla = mla_attention(mq, mk, mv, batch, seq)

        o_hg = hgrn2(p_hg, lower_bounds, hg_out_norm[l], l, batch, seq)

        merged = gated_merge(o_sb, o_mla, o_hg, w_branch_sb, w_branch_mla, w_branch_hg, l, p_gate)
        xf = weight_matmul(merged, w_out, l, F32, 1024, 512, res=xf)

        h2 = rmsnorm_rows(xf, ffn_norm[l])
        act = up_conv_glu(h2, w_up, ffn_conv, l, seq)
        for part in range(2):
            xf = weight_matmul(act, w_down, l, F32, 1024, 512, k_part=(part, 2), res=xf)
    return xf.reshape(batch, seq, d_model)
```

```python
import functools

import jax
import jax.numpy as jnp
from jax import lax
from jax.experimental import pallas as pl
from jax.experimental.pallas import tpu as pltpu

F32 = jnp.float32
BF16 = jnp.bfloat16

SUBLANES = 8
HEAD_DIM = 128
N_HEADS = 8
MLA_ROPE = 64
MLA_QK = HEAD_DIM + MLA_ROPE
MLA_PAD = 256
HG_CHUNK = 64
ROPE_THETA = 10000.0
EPS = 1e-6
MASK_NEG = -1e30
LB_FLOOR = 1e-30

VMEM_LIMIT_BYTES = 56 * 1024 * 1024


def _params(*sem):
    return pltpu.CompilerParams(dimension_semantics=sem, vmem_limit_bytes=VMEM_LIMIT_BYTES)


def _dot(a, b):
    return jnp.dot(a, b, preferred_element_type=F32)


def _dot_nt(a, b):
    return lax.dot_general(a, b, (((1,), (1,)), ((), ())), preferred_element_type=F32)


def _dot_tn(a, b):
    return lax.dot_general(a, b, (((0,), (0,)), ((), ())), preferred_element_type=F32)


def _rms(x, width):
    return x * lax.rsqrt(jnp.sum(x * x, axis=-1, keepdims=True) / width + EPS)


def _sigmoid(x):
    return 0.5 * jnp.tanh(0.5 * x) + 0.5


def _rmsnorm_kernel(x_ref, g_ref, o_ref):
    x = x_ref[...]
    o_ref[...] = (_rms(x, x.shape[-1]) * g_ref[...]).astype(o_ref.dtype)


def rmsnorm_rows(x, g, tm=256):
    m, d = x.shape
    return pl.pallas_call(
        _rmsnorm_kernel,
        grid=(m // tm,),
        in_specs=[pl.BlockSpec((tm, d), lambda i: (i, 0)),
                  pl.BlockSpec((1, d), lambda i: (0, 0))],
        out_specs=pl.BlockSpec((tm, d), lambda i: (i, 0)),
        out_shape=jax.ShapeDtypeStruct((m, d), BF16),
        compiler_params=_params("parallel"),
        name="rmsnorm_rows",
    )(x, g.reshape(1, d))


CAST_ELEMS = 512 * 512


def _stage_weight(dst_ref, w_ref):
    nrows, ncols = dst_ref.shape
    rows = min(nrows, CAST_ELEMS // ncols)

    def body(r, carry):
        sl = pl.ds(pl.multiple_of(r * rows, rows), rows)
        dst_ref[sl, :] = w_ref[sl, :].astype(BF16)
        return carry

    lax.fori_loop(0, nrows // rows, body, 0)


def _wmm_kernel(*refs, transposed, residual):
    a_ref, w_ref = refs[0], refs[1]
    r_ref = refs[2] if residual else None
    o_ref, wbf_ref = refs[-2], refs[-1]

    @pl.when(pl.program_id(1) == 0)
    def _():
        _stage_weight(wbf_ref, w_ref.at[0] if transposed else w_ref)

    d = (_dot_nt if transposed else _dot)(a_ref[...], wbf_ref[...])
    if residual:
        d = r_ref[...] + d
    o_ref[...] = d.astype(o_ref.dtype)


def weight_matmul(a, w, layer, out_dtype, tm, tn, *, transposed=False, col0=0, n=None, k_part=(0, 1), res=None):
    m = a.shape[0]
    kp, parts = k_part
    k = a.shape[1] // parts
    if transposed:
        assert parts == 1 and w.shape[2] == k
        n = w.shape[1] - col0 if n is None else n
        w_spec = pl.BlockSpec((pl.Element(1), pl.Element(tn), pl.Element(k)),
                              lambda j, i: (layer, pl.multiple_of(col0 + j * tn, SUBLANES), 0))
        staged = (tn, k)
    else:
        assert col0 % tn == 0 and w.shape[1] == k * parts
        n = w.shape[2] - col0 if n is None else n
        w_spec = pl.BlockSpec((None, k, tn), lambda j, i: (layer, kp, col0 // tn + j))
        staged = (k, tn)
    assert m % tm == 0 and n % tn == 0 and a.shape[1] == k * parts
    in_specs = [pl.BlockSpec((tm, k), lambda j, i: (i, kp)), w_spec]
    args = [a, w]
    if res is not None:
        in_specs.append(pl.BlockSpec((tm, tn), lambda j, i: (i, j)))
        args.append(res)
    return pl.pallas_call(
        functools.partial(_wmm_kernel, transposed=transposed, residual=res is not None),
        grid=(n // tn, m // tm),
        in_specs=in_specs,
        out_specs=pl.BlockSpec((tm, tn), lambda j, i: (i, j)),
        out_shape=jax.ShapeDtypeStruct((m, n), out_dtype),
        scratch_shapes=[pltpu.VMEM(staged, BF16)],
        compiler_params=_params("arbitrary", "arbitrary"),
        name="weight_matmul",
    )(*args)


def _headnorm_kernel(x_ref, g_ref, o_ref):
    g = g_ref[0]
    for h in range(x_ref.shape[1] // HEAD_DIM):
        sl = slice(h * HEAD_DIM, (h + 1) * HEAD_DIM)
        x = x_ref[:, sl].astype(F32)
        o_ref[:, sl] = (_rms(x, HEAD_DIM) * g).astype(o_ref.dtype)


def headnorm_qk(proj, gains, width, tm=512):
    m = proj.shape[0]
    return pl.pallas_call(
        _headnorm_kernel,
        grid=(m // tm, 2),
        in_specs=[pl.BlockSpec((tm, width), lambda i, j: (i, j)),
                  pl.BlockSpec((1, 1, HEAD_DIM), lambda i, j: (j, 0, 0))],
        out_specs=pl.BlockSpec((tm, width), lambda i, j: (i, j)),
        out_shape=jax.ShapeDtypeStruct((m, 2 * width), BF16),
        compiler_params=_params("parallel", "arbitrary"),
        name="headnorm_qk",
    )(proj, gains)


def _sb_block(q, ks, vs, carry, acc, from_here, scale, strict):
    z = _dot_nt(q, ks) * scale
    log_1m_beta = -(jnp.maximum(z, 0.0) + jnp.log(1.0 + jnp.exp(-jnp.abs(z))))
    if strict is not None:
        log_1m_beta = jnp.where(strict, log_1m_beta, 0.0)
    hi = log_1m_beta.astype(BF16)
    lo = (log_1m_beta - hi.astype(F32)).astype(BF16)
    incl = _dot(jnp.concatenate([hi, lo], axis=1), from_here)
    w = jnp.exp(z + incl + carry)
    if strict is not None:
        w = jnp.where(strict, w, 0.0)
    acc = acc + _dot(w.astype(BF16), vs)
    carry = carry + jnp.sum(log_1m_beta, axis=1, keepdims=True)
    return carry, acc


def _sb_attn_kernel(q_ref, k_ref, v_ref, o_ref, *, rows, kb, scale):
    qi = pl.program_id(2)
    per = rows // kb
    row = lax.broadcasted_iota(jnp.int32, (rows, kb), 0)
    col = lax.broadcasted_iota(jnp.int32, (rows, kb), 1)
    later = lax.broadcasted_iota(jnp.int32, (kb, kb), 0) >= lax.broadcasted_iota(jnp.int32, (kb, kb), 1)
    tri = later.astype(BF16)
    block = functools.partial(_sb_block, from_here=jnp.concatenate([tri, tri], axis=0), scale=scale)
    halves = (q_ref[:rows, :], q_ref[rows:, :])

    def kv(j):
        start = pl.multiple_of(j * kb, kb)
        return k_ref[pl.ds(start, kb), :], v_ref[pl.ds(start, kb), :]

    first_block = qi * 2 * per
    state = [(jnp.zeros((rows, 1), F32), jnp.zeros((rows, HEAD_DIM), F32)) for _ in halves]
    for d in reversed(range(2 * per)):
        ks, vs = kv(first_block + d)
        for c, q in enumerate(halves):
            start = c * rows
            if d * kb >= start + rows:
                continue
            strict = None if (d + 1) * kb <= start else (col + d * kb) < (row + start)
            state[c] = block(q, ks, vs, *state[c], strict=strict)

    def body(t, st):
        sa, sb = st
        for u in range(2):
            ks, vs = kv(first_block - 1 - 2 * t - u)
            sa = block(halves[0], ks, vs, *sa, strict=None)
            sb = block(halves[1], ks, vs, *sb, strict=None)
        return sa, sb

    sa, sb = lax.fori_loop(0, qi * per, body, tuple(state))
    o_ref[:rows, :] = sa[1].astype(o_ref.dtype)
    o_ref[rows:, :] = sb[1].astype(o_ref.dtype)


def sb_attention(qk, proj, batch, seq, rows=1024, kb=256):
    width = N_HEADS * HEAD_DIM
    assert seq % (2 * rows) == 0 and rows % kb == 0
    nq = seq // (2 * rows)
    kern = functools.partial(_sb_attn_kernel, rows=rows, kb=kb, scale=HEAD_DIM ** -0.5)
    return pl.pallas_call(
        kern,
        grid=(batch, N_HEADS, nq),
        in_specs=[pl.BlockSpec((2 * rows, HEAD_DIM), lambda b, h, i: (b * nq + i, h)),
                  pl.BlockSpec((seq, HEAD_DIM), lambda b, h, i: (b, N_HEADS + h)),
                  pl.BlockSpec((seq, HEAD_DIM), lambda b, h, i: (b, 2 * N_HEADS + h))],
        out_specs=pl.BlockSpec((2 * rows, HEAD_DIM), lambda b, h, i: (b * nq + i, h)),
        out_shape=jax.ShapeDtypeStruct((batch * seq, width), BF16),
        compiler_params=_params("parallel", "parallel", "arbitrary"),
        name="sb_attention",
    )(qk, qk, proj)


def _rope_lanes(x, cos_t, sin_t):
    lane = lax.broadcasted_iota(jnp.int32, x.shape, 1)
    half = MLA_ROPE // 2
    partner = jnp.where(lane < half, pltpu.roll(x, HEAD_DIM - half, 1), pltpu.roll(x, half, 1))
    return x * cos_t + partner * sin_t


def _mla_prep_kernel(p_ref, kr_ref, qa_ref, kva_ref, wq_ref, wkn_ref, wv_ref, gq_ref, gk_ref,
                     cos_ref, sin_ref, q_out, k_out, v_out, *, q_lora):
    cos_t = cos_ref[...]
    sin_t = sin_ref[...]
    gq = gq_ref[...]
    gk = gk_ref[...]
    cq = p_ref[:, :q_lora].astype(F32)
    cqn = (_rms(cq, q_lora) * qa_ref[...]).astype(BF16)
    qraw = _dot(cqn, wq_ref[...])
    for h in range(N_HEADS):
        blk = qraw[:, h * MLA_PAD:(h + 1) * MLA_PAD]
        y = _rms(blk, MLA_QK) * gq
        q_out[:, h * MLA_PAD:h * MLA_PAD + HEAD_DIM] = y[:, :HEAD_DIM].astype(q_out.dtype)
        q_out[:, h * MLA_PAD + HEAD_DIM:(h + 1) * MLA_PAD] = _rope_lanes(
            y[:, HEAD_DIM:], cos_t, sin_t).astype(q_out.dtype)

    ckv = p_ref[:, q_lora:].astype(F32)
    ckvn = (_rms(ckv, ckv.shape[-1]) * kva_ref[...]).astype(BF16)
    kn = _dot(ckvn, wkn_ref[...])
    v_out[...] = _dot(ckvn, wv_ref[...]).astype(v_out.dtype)
    kr = kr_ref[...]
    lane = lax.broadcasted_iota(jnp.int32, kr.shape, 1)
    kr = jnp.where(lane < MLA_ROPE, kr, 0.0)
    kr_ss = jnp.sum(kr * kr, axis=-1, keepdims=True)
    for h in range(N_HEADS):
        kb = kn[:, h * HEAD_DIM:(h + 1) * HEAD_DIM]
        inv = lax.rsqrt((jnp.sum(kb * kb, axis=-1, keepdims=True) + kr_ss) / MLA_QK + EPS)
        k_out[:, h * MLA_PAD:h * MLA_PAD + HEAD_DIM] = (kb * inv * gk[:, :HEAD_DIM]).astype(k_out.dtype)
        k_out[:, h * MLA_PAD + HEAD_DIM:(h + 1) * MLA_PAD] = _rope_lanes(
            kr * inv * gk[:, HEAD_DIM:], cos_t, sin_t).astype(k_out.dtype)


def mla_prep(proj, kr, qa, kva, wq, wkn, wv, gq, gk, cos_t, sin_t, seq, q_lora, kv_lora, tm=512):
    m = proj.shape[0]
    lat = q_lora + kv_lora
    lat_blk = (3 * N_HEADS * HEAD_DIM) // lat
    assert lat_blk * lat == 3 * N_HEADS * HEAD_DIM
    ns = seq // tm
    full = lambda i: (0, 0)
    kern = functools.partial(_mla_prep_kernel, q_lora=q_lora)
    return pl.pallas_call(
        kern,
        grid=(m // tm,),
        in_specs=[pl.BlockSpec((tm, lat), lambda i: (i, lat_blk)),
                  pl.BlockSpec((tm, HEAD_DIM), lambda i: (i, 0)),
                  pl.BlockSpec((1, q_lora), full),
                  pl.BlockSpec((1, kv_lora), full),
                  pl.BlockSpec(wq.shape, full),
                  pl.BlockSpec(wkn.shape, full),
                  pl.BlockSpec(wv.shape, full),
                  pl.BlockSpec((1, MLA_PAD), full),
                  pl.BlockSpec((1, MLA_PAD), full),
                  pl.BlockSpec((tm, HEAD_DIM), lambda i: (i % ns, 0)),
                  pl.BlockSpec((tm, HEAD_DIM), lambda i: (i % ns, 0))],
        out_specs=[pl.BlockSpec((tm, N_HEADS * MLA_PAD), lambda i: (i, 0)),
                   pl.BlockSpec((tm, N_HEADS * MLA_PAD), lambda i: (i, 0)),
                   pl.BlockSpec((tm, N_HEADS * HEAD_DIM), lambda i: (i, 0))],
        out_shape=[jax.ShapeDtypeStruct((m, N_HEADS * MLA_PAD), BF16),
                   jax.ShapeDtypeStruct((m, N_HEADS * MLA_PAD), BF16),
                   jax.ShapeDtypeStruct((m, N_HEADS * HEAD_DIM), BF16)],
        compiler_params=_params("parallel"),
        name="mla_prep",
    )(proj, kr, qa, kva, wq, wkn, wv, gq, gk, cos_t, sin_t)


def _softmax_block(q, ks, vs, state, scale, causal):
    m, l, acc = state
    z = _dot_nt(q, ks)
    if causal is not None:
        z = jnp.where(causal, z, MASK_NEG)
    m_new = jnp.maximum(m, jnp.max(z, axis=1, keepdims=True))
    p = jnp.exp((z - m_new) * scale)
    alpha = jnp.exp((m - m_new) * scale)
    l = alpha * l + jnp.sum(p, axis=1, keepdims=True)
    acc = alpha * acc + _dot(p.astype(BF16), vs)
    return m_new, l, acc


def _mla_attn_kernel(q_ref, k_ref, v_ref, o_ref, *, rows, kb, scale):
    qi = pl.program_id(2)
    per = rows // kb
    row = lax.broadcasted_iota(jnp.int32, (rows, kb), 0)
    col = lax.broadcasted_iota(jnp.int32, (rows, kb), 1)
    block = functools.partial(_softmax_block, scale=scale)
    halves = (q_ref[:rows, :], q_ref[rows:, :])
    init = (jnp.full((rows, 1), MASK_NEG, F32), jnp.zeros((rows, 1), F32), jnp.zeros((rows, HEAD_DIM), F32))

    def kv(j):
        start = pl.multiple_of(j * kb, kb)
        return k_ref[pl.ds(start, kb), :], v_ref[pl.ds(start, kb), :]

    def body(t, st):
        ks, vs = kv(t)
        return tuple(block(q, ks, vs, s, causal=None) for q, s in zip(halves, st))

    first_block = qi * 2 * per
    state = list(lax.fori_loop(0, first_block, body, (init, init)))
    for d in range(2 * per):
        ks, vs = kv(first_block + d)
        for c, q in enumerate(halves):
            start = c * rows
            if d * kb >= start + rows:
                continue
            causal = None if (d + 1) * kb <= start else (col + d * kb) <= (row + start)
            state[c] = block(q, ks, vs, state[c], causal=causal)
    for c, (_, l, acc) in enumerate(state):
        o_ref[c * rows:(c + 1) * rows, :] = (acc / l).astype(o_ref.dtype)


def mla_attention(q, k, v, batch, seq, rows=2048, kb=1024):
    assert seq % (2 * rows) == 0 and rows % kb == 0
    nq = seq // (2 * rows)
    kern = functools.partial(_mla_attn_kernel, rows=rows, kb=kb, scale=MLA_QK ** -0.5)
    return pl.pallas_call(
        kern,
        grid=(batch, N_HEADS, nq),
        in_specs=[pl.BlockSpec((2 * rows, MLA_PAD), lambda b, h, i: (b * nq + i, h)),
                  pl.BlockSpec((seq, MLA_PAD), lambda b, h, i: (b, h)),
                  pl.BlockSpec((seq, HEAD_DIM), lambda b, h, i: (b, h))],
        out_specs=pl.BlockSpec((2 * rows, HEAD_DIM), lambda b, h, i: (b * nq + i, h)),
        out_shape=jax.ShapeDtypeStruct((batch * seq, N_HEADS * HEAD_DIM), BF16),
        compiler_params=_params("parallel", "parallel", "arbitrary"),
        name="mla_attention",
    )(q, k, v)


def _seg_cumsum(x, seg, row):
    pos = row & (seg - 1)
    s = 1
    while s < seg:
        x = x + jnp.where(pos >= s, pltpu.roll(x, s, 0), 0.0)
        s *= 2
    return x


def _seg_rev_cumsum_excl(x, seg, row):
    pos = row & (seg - 1)
    n = x.shape[0]
    y = x
    s = 1
    while s < seg:
        y = y + jnp.where(pos < seg - s, pltpu.roll(y, n - s, 0), 0.0)
        s *= 2
    return y - x


def _rebase_halves(x, half, second):
    pieces = []
    for s in range(0, x.shape[0], 2 * half):
        pivot = x[s + half - 1:s + half]
        lo, hi = x[s:s + half], x[s + half:s + 2 * half]
        pieces += [lo, hi - pivot] if second else [lo - pivot, hi]
    return jnp.concatenate(pieces, axis=0)


def _hgrn2_kernel(lb_ref, q_ref, z_ref, i_ref, g_ref, gn_ref, o_ref, st_ref, *, rows, layer):
    @pl.when(pl.program_id(2) == 0)
    def _():
        st_ref[...] = jnp.zeros_like(st_ref)

    lbp = lb_ref[...]
    e = jnp.exp(lbp - jnp.max(lbp, axis=0, keepdims=True))
    p = e / jnp.sum(e, axis=0, keepdims=True)
    lb = jnp.sum(p[:layer + 1], axis=0, keepdims=True) - p[0:1]

    z = z_ref[...]
    q = q_ref[...]
    v = i_ref[...]
    log_sig = jnp.minimum(z, 0.0) - jnp.log(1.0 + jnp.exp(-jnp.abs(z)))
    a = jnp.log(jnp.maximum(lb, LB_FLOOR))
    c = jnp.log1p(-lb) + log_sig
    log_f = jnp.maximum(a, c) + jnp.log(1.0 + jnp.exp(-jnp.abs(a - c)))
    k = 1.0 - jnp.exp(log_f)

    row = lax.broadcasted_iota(jnp.int32, (rows, HEAD_DIM), 0)
    fwd = {HG_CHUNK: _seg_cumsum(log_f, HG_CHUNK, row)}
    rev = {HG_CHUNK: _seg_rev_cumsum_excl(log_f, HG_CHUNK, row)}
    half = HG_CHUNK // 2
    while half >= 1:
        if half >= SUBLANES:
            fwd[half] = _rebase_halves(fwd[2 * half], half, second=True)
            rev[half] = _rebase_halves(rev[2 * half], half, second=False)
        else:
            fwd[half] = _seg_cumsum(log_f, half, row)
            rev[half] = _seg_rev_cumsum_excl(log_f, half, row)
        half //= 2
    b = fwd[HG_CHUNK]
    q_in = (q * jnp.exp(b)).astype(BF16)
    k_out = (k * jnp.exp(rev[HG_CHUNK])).astype(BF16)
    v16 = v.astype(BF16)
    diag = jnp.sum(q * k, axis=1, keepdims=True)

    levels = []
    half = 1
    while half < HG_CHUNK:
        second = (row & (2 * half - 1)) >= half
        ql = jnp.where(second, q * jnp.exp(fwd[half]), 0.0).astype(BF16)
        kl = jnp.where(second, 0.0, k * jnp.exp(rev[half])).astype(BF16)
        levels.append((2 * half, ql, kl))
        half *= 2

    ct = lax.broadcasted_iota(jnp.int32, (HG_CHUNK, HG_CHUNK), 0)
    cs = lax.broadcasted_iota(jnp.int32, (HG_CHUNK, HG_CHUNK), 1)
    st = st_ref[...]
    outs = []
    for ci in range(rows // HG_CHUNK):
        sl = slice(ci * HG_CHUNK, (ci + 1) * HG_CHUNK)
        attn = jnp.zeros((HG_CHUNK, HG_CHUNK), F32)
        for w, ql, kl in levels:
            part = _dot_nt(ql[sl], kl[sl])
            if w < HG_CHUNK:
                part = jnp.where((ct & -w) == (cs & -w), part, 0.0)
            attn = attn + part
        o = _dot(attn.astype(BF16), v16[sl]) + diag[sl] * v[sl] + _dot_nt(q_in[sl], st.astype(BF16))
        outs.append(o)
        b_last = b[(ci + 1) * HG_CHUNK - 1:(ci + 1) * HG_CHUNK, :]
        st = st * jnp.exp(b_last) + _dot_tn(v16[sl], k_out[sl])
    st_ref[...] = st

    o = jnp.concatenate(outs, axis=0)
    g = g_ref[...]
    o_ref[...] = (_rms(o, HEAD_DIM) * gn_ref[...] * (g * _sigmoid(g))).astype(o_ref.dtype)


def hgrn2(proj, lower_bounds, out_norm, layer, batch, seq, rows=2048):
    depth = lower_bounds.shape[0]
    assert seq % rows == 0 and rows % HG_CHUNK == 0
    nr = seq // rows
    kern = functools.partial(_hgrn2_kernel, rows=rows, layer=layer)

    def col(c):
        return pl.BlockSpec((rows, HEAD_DIM), lambda b, h, r: (b * nr + r, c * N_HEADS + h))

    return pl.pallas_call(
        kern,
        grid=(batch, N_HEADS, nr),
        in_specs=[pl.BlockSpec((depth, HEAD_DIM), lambda b, h, r: (0, h)),
                  col(0), col(1), col(2), col(3),
                  pl.BlockSpec((1, HEAD_DIM), lambda b, h, r: (0, 0))],
        out_specs=pl.BlockSpec((rows, HEAD_DIM), lambda b, h, r: (b * nr + r, h)),
        out_shape=jax.ShapeDtypeStruct((batch * seq, N_HEADS * HEAD_DIM), BF16),
        scratch_shapes=[pltpu.VMEM((HEAD_DIM, HEAD_DIM), F32)],
        compiler_params=_params("parallel", "parallel", "arbitrary"),
        name="hgrn2",
    )(lower_bounds, proj, proj, proj, proj, out_norm.reshape(1, HEAD_DIM))


def _merge_kernel(a0, a1, a2, w0, w1, w2, g0, g1, g2, o_ref, wbf_ref):
    @pl.when(pl.program_id(1) == 0)
    def _():
        for n, w_ref in enumerate((w0, w1, w2)):
            _stage_weight(wbf_ref.at[n], w_ref)

    acc = _sigmoid(g0[...].astype(F32)) * _dot(a0[...], wbf_ref[0])
    acc = acc + _sigmoid(g1[...].astype(F32)) * _dot(a1[...], wbf_ref[1])
    acc = acc + _sigmoid(g2[...].astype(F32)) * _dot(a2[...], wbf_ref[2])
    o_ref[...] = acc.astype(o_ref.dtype)


def gated_merge(o_sb, o_mla, o_hg, w_sb, w_mla, w_hg, layer, gate_logits, tm=1024, tn=512):
    m, k = o_sb.shape
    n = w_sb.shape[2]
    nb = n // tn
    a_spec = pl.BlockSpec((tm, k), lambda j, i: (i, 0))
    w_spec = pl.BlockSpec((None, k, tn), lambda j, i: (layer, 0, j))

    def g_spec(branch):
        return pl.BlockSpec((tm, tn), lambda j, i: (i, branch * nb + j))

    return pl.pallas_call(
        _merge_kernel,
        grid=(nb, m // tm),
        in_specs=[a_spec, a_spec, a_spec, w_spec, w_spec, w_spec, g_spec(0), g_spec(1), g_spec(2)],
        out_specs=pl.BlockSpec((tm, tn), lambda j, i: (i, j)),
        out_shape=jax.ShapeDtypeStruct((m, n), BF16),
        scratch_shapes=[pltpu.VMEM((3, k, tn), BF16)],
        compiler_params=_params("arbitrary", "arbitrary"),
        name="gated_merge",
    )(o_sb, o_mla, o_hg, w_sb, w_mla, w_hg, gate_logits, gate_logits, gate_logits)


HALO = 16

def _up_conv_glu_kernel(a_ref, ah_ref, wg_ref, wv_ref, cg_ref, cv_ref, o_ref, wbf_ref, *, tiles_per_seq):
    @pl.when(pl.program_id(1) == 0)
    def _():
        _stage_weight(wbf_ref.at[0], wg_ref)
        _stage_weight(wbf_ref.at[1], wv_ref)

    a = a_ref[...]
    ah = ah_ref[...]
    seq_start = (pl.program_id(1) % tiles_per_seq) == 0
    row = lax.broadcasted_iota(jnp.int32, o_ref.shape, 0)

    def conv(slot, c_ref):
        w = wbf_ref[slot]
        u = _dot(a, w)
        halo = jnp.where(seq_start, 0.0, _dot(ah, w))
        prev1 = halo[HALO - 1:HALO]
        prev2 = halo[HALO - 2:HALO - 1]
        u1 = jnp.where(row == 0, prev1, pltpu.roll(u, 1, 0))
        u2 = jnp.where(row == 0, prev2, jnp.where(row == 1, prev1, pltpu.roll(u, 2, 0)))
        c = c_ref[...]
        return u2 * c[0:1] + u1 * c[1:2] + u * c[2:3]

    gate = conv(0, cg_ref)
    val = conv(1, cv_ref)
    o_ref[...] = (gate * _sigmoid(gate) * val).astype(o_ref.dtype)


def up_conv_glu(h, w_up, conv_w, layer, seq, tm=1024, tn=256):
    m, k = h.shape
    d_ff = w_up.shape[2] // 2
    nb = d_ff // tn
    kern = functools.partial(_up_conv_glu_kernel, tiles_per_seq=seq // tm)
    halo_blocks = tm // HALO
    return pl.pallas_call(
        kern,
        grid=(nb, m // tm),
        in_specs=[pl.BlockSpec((tm, k), lambda j, i: (i, 0)),
                  pl.BlockSpec((HALO, k), lambda j, i: (jnp.maximum(i * halo_blocks - 1, 0), 0)),
                  pl.BlockSpec((None, k, tn), lambda j, i: (layer, 0, j)),
                  pl.BlockSpec((None, k, tn), lambda j, i: (layer, 0, nb + j)),
                  pl.BlockSpec((None, conv_w.shape[1], tn), lambda j, i: (layer, 0, j)),
                  pl.BlockSpec((None, conv_w.shape[1], tn), lambda j, i: (layer, 0, nb + j))],
        out_specs=pl.BlockSpec((tm, tn), lambda j, i: (i, j)),
        out_shape=jax.ShapeDtypeStruct((m, d_ff), BF16),
        scratch_shapes=[pltpu.VMEM((2, k, tn), BF16)],
        compiler_params=_params("arbitrary", "arbitrary"),
        name="up_conv_glu",
    )(h, h, w_up, w_up, conv_w, conv_w)


def _rope_lane_tables(seq):
    pos = jnp.arange(seq, dtype=F32)
    inv = ROPE_THETA ** (-jnp.arange(0, MLA_ROPE, 2, dtype=F32) / MLA_ROPE)
    ang = pos[:, None] * inv[None, :]
    cos, sin = jnp.cos(ang), jnp.sin(ang)
    zeros = jnp.zeros((seq, HEAD_DIM - MLA_ROPE), F32)
    return (jnp.concatenate([cos, cos, zeros], axis=1),
            jnp.concatenate([-sin, sin, zeros], axis=1))


def _pad_head_gain(g):
    return jnp.pad(g, (0, MLA_PAD - MLA_QK)).reshape(1, MLA_PAD)


def kernel(x, lower_bounds, attn_norm, w_in, sb_q_norm, sb_k_norm, mla_q_a_norm, mla_kv_a_norm, mla_w_q_b, mla_w_kv_b, mla_q_norm, mla_k_norm, hg_out_norm, w_branch_sb, w_branch_mla, w_branch_hg, w_out, ffn_norm, w_up, ffn_conv, w_down):
    batch, seq, d_model = x.shape
    depth = w_in.shape[0]
    width = N_HEADS * HEAD_DIM
    q_lora = mla_w_q_b.shape[1]
    kv_lora = mla_w_kv_b.shape[1]
    c_attn = 3 * width + q_lora + kv_lora
    c_hg = c_attn + MLA_ROPE
    c_gate = c_hg + 4 * width

    cos_t, sin_t = _rope_lane_tables(seq)
    xf = x.reshape(batch * seq, d_model)
    w_in_t = jnp.swapaxes(w_in, 1, 2)
    for l in range(depth):
        wq = jnp.pad(mla_w_q_b[l], ((0, 0), (0, 0), (0, MLA_PAD - MLA_QK))).reshape(q_lora, N_HEADS * MLA_PAD)
        wkn = mla_w_kv_b[l][:, :, :HEAD_DIM].reshape(kv_lora, width)
        wv = mla_w_kv_b[l][:, :, HEAD_DIM:].reshape(kv_lora, width)

        h = rmsnorm_rows(xf, attn_norm[l])
        p_attn = weight_matmul(h, w_in_t, l, BF16, 1024, 512, transposed=True, n=c_attn)
        p_kr = weight_matmul(h, w_in_t, l, F32, 1024, HEAD_DIM, transposed=True, col0=c_attn, n=HEAD_DIM)
        p_hg = weight_matmul(h, w_in_t, l, F32, 1024, 512, transposed=True, col0=c_hg, n=c_gate - c_hg)
        p_gate = weight_matmul(h, w_in_t, l, BF16, 1024, 512, transposed=True, col0=c_gate)

        sb_qk = headnorm_qk(p_attn, jnp.stack([sb_q_norm[l], sb_k_norm[l]]).reshape(2, 1, HEAD_DIM), width)
        o_sb = sb_attention(sb_qk, p_attn, batch, seq)

        mq, mk, mv = mla_prep(p_attn, p_kr, mla_q_a_norm[l].reshape(1, q_lora),
                              mla_kv_a_norm[l].reshape(1, kv_lora), wq.astype(BF16), wkn.astype(BF16),
                              wv.astype(BF16), _pad_head_gain(mla_q_norm[l]), _pad_head_gain(mla_k_norm[l]),
                              cos_t, sin_t, seq, q_lora, kv_lora)
        o_mla = mla_attention(mq, mk, mv, batch, seq)

        o_hg = hgrn2(p_hg, lower_bounds, hg_out_norm[l], l, batch, seq)

        merged = gated_merge(o_sb, o_mla, o_hg, w_branch_sb, w_branch_mla, w_branch_hg, l, p_gate)
        xf = weight_matmul(merged, w_out, l, F32, 1024, 512, res=xf)

        h2 = rmsnorm_rows(xf, ffn_norm[l])
        act = up_conv_glu(h2, w_up, ffn_conv, l, seq)
        for part in range(2):
            xf = weight_matmul(act, w_down, l, F32, 1024, 512, k_part=(part, 2), res=xf)
    return xf.reshape(batch, seq, d_model)
```

```python
import functools

import jax
import jax.numpy as jnp
from jax import lax
from jax.experimental import pallas as pl
from jax.experimental.pallas import tpu as pltpu

F32 = jnp.float32
BF16 = jnp.bfloat16

SUBLANES = 8
HEAD_DIM = 128
N_HEADS = 8
MLA_ROPE = 64
MLA_QK = HEAD_DIM + MLA_ROPE
MLA_PAD = 256
HG_CHUNK = 64
ROPE_THETA = 10000.0
EPS = 1e-6
MASK_NEG = -1e30
LB_FLOOR = 1e-30

VMEM_LIMIT_BYTES = 56 * 1024 * 1024


def _params(*sem):
    return pltpu.CompilerParams(dimension_semantics=sem, vmem_limit_bytes=VMEM_LIMIT_BYTES)


def _dot(a, b):
    return jnp.dot(a, b, preferred_element_type=F32)


def _dot_nt(a, b):
    return lax.dot_general(a, b, (((1,), (1,)), ((), ())), preferred_element_type=F32)


def _dot_tn(a, b):
    return lax.dot_general(a, b, (((0,), (0,)), ((), ())), preferred_element_type=F32)


def _rms(x, width):
    return x * lax.rsqrt(jnp.sum(x * x, axis=-1, keepdims=True) / width + EPS)


def _sigmoid(x):
    return 0.5 * jnp.tanh(0.5 * x) + 0.5


def _rmsnorm_kernel(x_ref, g_ref, o_ref):
    x = x_ref[...]
    o_ref[...] = (_rms(x, x.shape[-1]) * g_ref[...]).astype(o_ref.dtype)


def rmsnorm_rows(x, g, tm=256):
    m, d = x.shape
    return pl.pallas_call(
        _rmsnorm_kernel,
        grid=(m // tm,),
        in_specs=[pl.BlockSpec((tm, d), lambda i: (i, 0)),
                  pl.BlockSpec((1, d), lambda i: (0, 0))],
        out_specs=pl.BlockSpec((tm, d), lambda i: (i, 0)),
        out_shape=jax.ShapeDtypeStruct((m, d), BF16),
        compiler_params=_params("parallel"),
        name="rmsnorm_rows",
    )(x, g.reshape(1, d))


CAST_ELEMS = 512 * 512


def _stage_weight(dst_ref, w_ref):
    nrows, ncols = dst_ref.shape
    rows = min(nrows, CAST_ELEMS // ncols)

    def body(r, carry):
        sl = pl.ds(pl.multiple_of(r * rows, rows), rows)
        dst_ref[sl, :] = w_ref[sl, :].astype(BF16)
        return carry

    lax.fori_loop(0, nrows // rows, body, 0)


def _wmm_kernel(*refs, transposed, residual):
    a_ref, w_ref = refs[0], refs[1]
    r_ref = refs[2] if residual else None
    o_ref, wbf_ref = refs[-2], refs[-1]

    @pl.when(pl.program_id(1) == 0)
    def _():
        _stage_weight(wbf_ref, w_ref.at[0] if transposed else w_ref)

    d = (_dot_nt if transposed else _dot)(a_ref[...], wbf_ref[...])
    if residual:
        d = r_ref[...] + d
    o_ref[...] = d.astype(o_ref.dtype)


def weight_matmul(a, w, layer, out_dtype, tm, tn, *, transposed=False, col0=0, n=None, k_part=(0, 1), res=None):
    m = a.shape[0]
    kp, parts = k_part
    k = a.shape[1] // parts
    if transposed:
        assert parts == 1 and w.shape[2] == k
        n = w.shape[1] - col0 if n is None else n
        w_spec = pl.BlockSpec((pl.Element(1), pl.Element(tn), pl.Element(k)),
                              lambda j, i: (layer, pl.multiple_of(col0 + j * tn, SUBLANES), 0))
        staged = (tn, k)
    else:
        assert col0 % tn == 0 and w.shape[1] == k * parts
        n = w.shape[2] - col0 if n is None else n
        w_spec = pl.BlockSpec((None, k, tn), lambda j, i: (layer, kp, col0 // tn + j))
        staged = (k, tn)
    assert m % tm == 0 and n % tn == 0 and a.shape[1] == k * parts
    in_specs = [pl.BlockSpec((tm, k), lambda j, i: (i, kp)), w_spec]
    args = [a, w]
    if res is not None:
        in_specs.append(pl.BlockSpec((tm, tn), lambda j, i: (i, j)))
        args.append(res)
    return pl.pallas_call(
        functools.partial(_wmm_kernel, transposed=transposed, residual=res is not None),
        grid=(n // tn, m // tm),
        in_specs=in_specs,
        out_specs=pl.BlockSpec((tm, tn), lambda j, i: (i, j)),
        out_shape=jax.ShapeDtypeStruct((m, n), out_dtype),
        scratch_shapes=[pltpu.VMEM(staged, BF16)],
        compiler_params=_params("arbitrary", "arbitrary"),
        name="weight_matmul",
    )(*args)


def _headnorm_kernel(x_ref, g_ref, o_ref):
    g = g_ref[0]
    for h in range(x_ref.shape[1] // HEAD_DIM):
        sl = slice(h * HEAD_DIM, (h + 1) * HEAD_DIM)
        x = x_ref[:, sl].astype(F32)
        o_ref[:, sl] = (_rms(x, HEAD_DIM) * g).astype(o_ref.dtype)


def headnorm_qk(proj, gains, width, tm=1024):
    m = proj.shape[0]
    return pl.pallas_call(
        _headnorm_kernel,
        grid=(m // tm, 2),
        in_specs=[pl.BlockSpec((tm, width), lambda i, j: (i, j)),
                  pl.BlockSpec((1, 1, HEAD_DIM), lambda i, j: (j, 0, 0))],
        out_specs=pl.BlockSpec((tm, width), lambda i, j: (i, j)),
        out_shape=jax.ShapeDtypeStruct((m, 2 * width), BF16),
        compiler_params=_params("parallel", "arbitrary"),
        name="headnorm_qk",
    )(proj, gains)


def _sb_block(q, ks, vs, carry, acc, from_here, scale, strict):
    z = _dot_nt(q, ks) * scale
    log_1m_beta = -(jnp.maximum(z, 0.0) + jnp.log(1.0 + jnp.exp(-jnp.abs(z))))
    if strict is not None:
        log_1m_beta = jnp.where(strict, log_1m_beta, 0.0)
    hi = log_1m_beta.astype(BF16)
    lo = (log_1m_beta - hi.astype(F32)).astype(BF16)
    incl = _dot(jnp.concatenate([hi, lo], axis=1), from_here)
    w = jnp.exp(z + incl + carry)
    if strict is not None:
        w = jnp.where(strict, w, 0.0)
    acc = acc + _dot(w.astype(BF16), vs)
    carry = carry + jnp.sum(log_1m_beta, axis=1, keepdims=True)
    return carry, acc


def _sb_attn_kernel(q_ref, k_ref, v_ref, o_ref, *, rows, kb, scale):
    qi = pl.program_id(2)
    per = rows // kb
    row = lax.broadcasted_iota(jnp.int32, (rows, kb), 0)
    col = lax.broadcasted_iota(jnp.int32, (rows, kb), 1)
    later = lax.broadcasted_iota(jnp.int32, (kb, kb), 0) >= lax.broadcasted_iota(jnp.int32, (kb, kb), 1)
    tri = later.astype(BF16)
    block = functools.partial(_sb_block, from_here=jnp.concatenate([tri, tri], axis=0), scale=scale)
    halves = (q_ref[:rows, :], q_ref[rows:, :])

    def kv(j):
        start = pl.multiple_of(j * kb, kb)
        return k_ref[pl.ds(start, kb), :], v_ref[pl.ds(start, kb), :]

    first_block = qi * 2 * per
    state = [(jnp.zeros((rows, 1), F32), jnp.zeros((rows, HEAD_DIM), F32)) for _ in halves]
    for d in reversed(range(2 * per)):
        ks, vs = kv(first_block + d)
        for c, q in enumerate(halves):
            start = c * rows
            if d * kb >= start + rows:
                continue
            strict = None if (d + 1) * kb <= start else (col + d * kb) < (row + start)
            state[c] = block(q, ks, vs, *state[c], strict=strict)

    def body(t, st):
        sa, sb = st
        for u in range(2):
            ks, vs = kv(first_block - 1 - 2 * t - u)
            sa = block(halves[0], ks, vs, *sa, strict=None)
            sb = block(halves[1], ks, vs, *sb, strict=None)
        return sa, sb

    sa, sb = lax.fori_loop(0, qi * per, body, tuple(state))
    o_ref[:rows, :] = sa[1].astype(o_ref.dtype)
    o_ref[rows:, :] = sb[1].astype(o_ref.dtype)


def sb_attention(qk, proj, batch, seq, rows=1024, kb=256):
    width = N_HEADS * HEAD_DIM
    assert seq % (2 * rows) == 0 and rows % kb == 0
    nq = seq // (2 * rows)
    kern = functools.partial(_sb_attn_kernel, rows=rows, kb=kb, scale=HEAD_DIM ** -0.5)
    return pl.pallas_call(
        kern,
        grid=(batch, N_HEADS, nq),
        in_specs=[pl.BlockSpec((2 * rows, HEAD_DIM), lambda b, h, i: (b * nq + i, h)),
                  pl.BlockSpec((seq, HEAD_DIM), lambda b, h, i: (b, N_HEADS + h)),
                  pl.BlockSpec((seq, HEAD_DIM), lambda b, h, i: (b, 2 * N_HEADS + h))],
        out_specs=pl.BlockSpec((2 * rows, HEAD_DIM), lambda b, h, i: (b * nq + i, h)),
        out_shape=jax.ShapeDtypeStruct((batch * seq, width), BF16),
        compiler_params=_params("parallel", "parallel", "arbitrary"),
        name="sb_attention",
    )(qk, qk, proj)


def _rope_lanes(x, cos_t, sin_t):
    lane = lax.broadcasted_iota(jnp.int32, x.shape, 1)
    half = MLA_ROPE // 2
    partner = jnp.where(lane < half, pltpu.roll(x, HEAD_DIM - half, 1), pltpu.roll(x, half, 1))
    return x * cos_t + partner * sin_t


def _mla_prep_kernel(p_ref, kr_ref, qa_ref, kva_ref, wq_ref, wkn_ref, wv_ref, gq_ref, gk_ref,
                     cos_ref, sin_ref, q_out, k_out, v_out, *, q_lora):
    cos_t = cos_ref[...]
    sin_t = sin_ref[...]
    gq = gq_ref[...]
    gk = gk_ref[...]
    cq = p_ref[:, :q_lora].astype(F32)
    cqn = (_rms(cq, q_lora) * qa_ref[...]).astype(BF16)
    qraw = _dot(cqn, wq_ref[...])
    for h in range(N_HEADS):
        blk = qraw[:, h * MLA_PAD:(h + 1) * MLA_PAD]
        y = _rms(blk, MLA_QK) * gq
        q_out[:, h * MLA_PAD:h * MLA_PAD + HEAD_DIM] = y[:, :HEAD_DIM].astype(q_out.dtype)
        q_out[:, h * MLA_PAD + HEAD_DIM:(h + 1) * MLA_PAD] = _rope_lanes(
            y[:, HEAD_DIM:], cos_t, sin_t).astype(q_out.dtype)

    ckv = p_ref[:, q_lora:].astype(F32)
    ckvn = (_rms(ckv, ckv.shape[-1]) * kva_ref[...]).astype(BF16)
    kn = _dot(ckvn, wkn_ref[...])
    v_out[...] = _dot(ckvn, wv_ref[...]).astype(v_out.dtype)
    kr = kr_ref[...]
    lane = lax.broadcasted_iota(jnp.int32, kr.shape, 1)
    kr = jnp.where(lane < MLA_ROPE, kr, 0.0)
    kr_ss = jnp.sum(kr * kr, axis=-1, keepdims=True)
    for h in range(N_HEADS):
        kb = kn[:, h * HEAD_DIM:(h + 1) * HEAD_DIM]
        inv = lax.rsqrt((jnp.sum(kb * kb, axis=-1, keepdims=True) + kr_ss) / MLA_QK + EPS)
        k_out[:, h * MLA_PAD:h * MLA_PAD + HEAD_DIM] = (kb * inv * gk[:, :HEAD_DIM]).astype(k_out.dtype)
        k_out[:, h * MLA_PAD + HEAD_DIM:(h + 1) * MLA_PAD] = _rope_lanes(
            kr * inv * gk[:, HEAD_DIM:], cos_t, sin_t).astype(k_out.dtype)


def mla_prep(proj, kr, qa, kva, wq, wkn, wv, gq, gk, cos_t, sin_t, seq, q_lora, kv_lora, tm=512):
    m = proj.shape[0]
    lat = q_lora + kv_lora
    lat_blk = (3 * N_HEADS * HEAD_DIM) // lat
    assert lat_blk * lat == 3 * N_HEADS * HEAD_DIM
    ns = seq // tm
    full = lambda i: (0, 0)
    kern = functools.partial(_mla_prep_kernel, q_lora=q_lora)
    return pl.pallas_call(
        kern,
        grid=(m // tm,),
        in_specs=[pl.BlockSpec((tm, lat), lambda i: (i, lat_blk)),
                  pl.BlockSpec((tm, HEAD_DIM), lambda i: (i, 0)),
                  pl.BlockSpec((1, q_lora), full),
                  pl.BlockSpec((1, kv_lora), full),
                  pl.BlockSpec(wq.shape, full),
                  pl.BlockSpec(wkn.shape, full),
                  pl.BlockSpec(wv.shape, full),
                  pl.BlockSpec((1, MLA_PAD), full),
                  pl.BlockSpec((1, MLA_PAD), full),
                  pl.BlockSpec((tm, HEAD_DIM), lambda i: (i % ns, 0)),
                  pl.BlockSpec((tm, HEAD_DIM), lambda i: (i % ns, 0))],
        out_specs=[pl.BlockSpec((tm, N_HEADS * MLA_PAD), lambda i: (i, 0)),
                   pl.BlockSpec((tm, N_HEADS * MLA_PAD), lambda i: (i, 0)),
                   pl.BlockSpec((tm, N_HEADS * HEAD_DIM), lambda i: (i, 0))],
        out_shape=[jax.ShapeDtypeStruct((m, N_HEADS * MLA_PAD), BF16),
                   jax.ShapeDtypeStruct((m, N_HEADS * MLA_PAD), BF16),
                   jax.ShapeDtypeStruct((m, N_HEADS * HEAD_DIM), BF16)],
        compiler_params=_params("parallel"),
        name="mla_prep",
    )(proj, kr, qa, kva, wq, wkn, wv, gq, gk, cos_t, sin_t)


def _softmax_block(q, ks, vs, state, scale, causal):
    m, l, acc = state
    z = _dot_nt(q, ks)
    if causal is not None:
        z = jnp.where(causal, z, MASK_NEG)
    m_new = jnp.maximum(m, jnp.max(z, axis=1, keepdims=True))
    p = jnp.exp((z - m_new) * scale)
    alpha = jnp.exp((m - m_new) * scale)
    l = alpha * l + jnp.sum(p, axis=1, keepdims=True)
    acc = alpha * acc + _dot(p.astype(BF16), vs)
    return m_new, l, acc


def _mla_attn_kernel(q_ref, k_ref, v_ref, o_ref, *, rows, kb, scale):
    qi = pl.program_id(2)
    per = rows // kb
    row = lax.broadcasted_iota(jnp.int32, (rows, kb), 0)
    col = lax.broadcasted_iota(jnp.int32, (rows, kb), 1)
    block = functools.partial(_softmax_block, scale=scale)
    halves = (q_ref[:rows, :], q_ref[rows:, :])
    init = (jnp.full((rows, 1), MASK_NEG, F32), jnp.zeros((rows, 1), F32), jnp.zeros((rows, HEAD_DIM), F32))

    def kv(j):
        start = pl.multiple_of(j * kb, kb)
        return k_ref[pl.ds(start, kb), :], v_ref[pl.ds(start, kb), :]

    def body(t, st):
        ks, vs = kv(t)
        return tuple(block(q, ks, vs, s, causal=None) for q, s in zip(halves, st))

    first_block = qi * 2 * per
    state = list(lax.fori_loop(0, first_block, body, (init, init)))
    for d in range(2 * per):
        ks, vs = kv(first_block + d)
        for c, q in enumerate(halves):
            start = c * rows
            if d * kb >= start + rows:
                continue
            causal = None if (d + 1) * kb <= start else (col + d * kb) <= (row + start)
            state[c] = block(q, ks, vs, state[c], causal=causal)
    for c, (_, l, acc) in enumerate(state):
        o_ref[c * rows:(c + 1) * rows, :] = (acc / l).astype(o_ref.dtype)


def mla_attention(q, k, v, batch, seq, rows=2048, kb=1024):
    assert seq % (2 * rows) == 0 and rows % kb == 0
    nq = seq // (2 * rows)
    kern = functools.partial(_mla_attn_kernel, rows=rows, kb=kb, scale=MLA_QK ** -0.5)
    return pl.pallas_call(
        kern,
        grid=(batch, N_HEADS, nq),
        in_specs=[pl.BlockSpec((2 * rows, MLA_PAD), lambda b, h, i: (b * nq + i, h)),
                  pl.BlockSpec((seq, MLA_PAD), lambda b, h, i: (b, h)),
                  pl.BlockSpec((seq, HEAD_DIM), lambda b, h, i: (b, h))],
        out_specs=pl.BlockSpec((2 * rows, HEAD_DIM), lambda b, h, i: (b * nq + i, h)),
        out_shape=jax.ShapeDtypeStruct((batch * seq, N_HEADS * HEAD_DIM), BF16),
        compiler_params=_params("parallel", "parallel", "arbitrary"),
        name="mla_attention",
    )(q, k, v)


def _seg_cumsum(x, seg, row):
    pos = row & (seg - 1)
    s = 1
    while s < seg:
        x = x + jnp.where(pos >= s, pltpu.roll(x, s, 0), 0.0)
        s *= 2
    return x


def _seg_rev_cumsum_excl(x, seg, row):
    pos = row & (seg - 1)
    n = x.shape[0]
    y = x
    s = 1
    while s < seg:
        y = y + jnp.where(pos < seg - s, pltpu.roll(y, n - s, 0), 0.0)
        s *= 2
    return y - x


def _rebase_halves(x, half, second):
    pieces = []
    for s in range(0, x.shape[0], 2 * half):
        pivot = x[s + half - 1:s + half]
        lo, hi = x[s:s + half], x[s + half:s + 2 * half]
        pieces += [lo, hi - pivot] if second else [lo - pivot, hi]
    return jnp.concatenate(pieces, axis=0)


def _hgrn2_kernel(lb_ref, q_ref, z_ref, i_ref, g_ref, gn_ref, o_ref, st_ref, *, rows, layer):
    @pl.when(pl.program_id(2) == 0)
    def _():
        st_ref[...] = jnp.zeros_like(st_ref)

    lbp = lb_ref[...]
    e = jnp.exp(lbp - jnp.max(lbp, axis=0, keepdims=True))
    p = e / jnp.sum(e, axis=0, keepdims=True)
    lb = jnp.sum(p[:layer + 1], axis=0, keepdims=True) - p[0:1]

    z = z_ref[...]
    q = q_ref[...]
    v = i_ref[...]
    log_sig = jnp.minimum(z, 0.0) - jnp.log(1.0 + jnp.exp(-jnp.abs(z)))
    a = jnp.log(jnp.maximum(lb, LB_FLOOR))
    c = jnp.log1p(-lb) + log_sig
    log_f = jnp.maximum(a, c) + jnp.log(1.0 + jnp.exp(-jnp.abs(a - c)))
    k = 1.0 - jnp.exp(log_f)

    row = lax.broadcasted_iota(jnp.int32, (rows, HEAD_DIM), 0)
    fwd = {HG_CHUNK: _seg_cumsum(log_f, HG_CHUNK, row)}
    rev = {HG_CHUNK: _seg_rev_cumsum_excl(log_f, HG_CHUNK, row)}
    half = HG_CHUNK // 2
    while half >= 1:
        if half >= SUBLANES:
            fwd[half] = _rebase_halves(fwd[2 * half], half, second=True)
            rev[half] = _rebase_halves(rev[2 * half], half, second=False)
        else:
            fwd[half] = _seg_cumsum(log_f, half, row)
            rev[half] = _seg_rev_cumsum_excl(log_f, half, row)
        half //= 2
    b = fwd[HG_CHUNK]
    q_in = (q * jnp.exp(b)).astype(BF16)
    k_out = (k * jnp.exp(rev[HG_CHUNK])).astype(BF16)
    v16 = v.astype(BF16)
    diag = jnp.sum(q * k, axis=1, keepdims=True)

    levels = []
    half = 1
    while half < HG_CHUNK:
        second = (row & (2 * half - 1)) >= half
        ql = jnp.where(second, q * jnp.exp(fwd[half]), 0.0).astype(BF16)
        kl = jnp.where(second, 0.0, k * jnp.exp(rev[half])).astype(BF16)
        levels.append((2 * half, ql, kl))
        half *= 2

    ct = lax.broadcasted_iota(jnp.int32, (HG_CHUNK, HG_CHUNK), 0)
    cs = lax.broadcasted_iota(jnp.int32, (HG_CHUNK, HG_CHUNK), 1)
    st = st_ref[...]
    outs = []
    for ci in range(rows // HG_CHUNK):
        sl = slice(ci * HG_CHUNK, (ci + 1) * HG_CHUNK)
        attn = jnp.zeros((HG_CHUNK, HG_CHUNK), F32)
        for w, ql, kl in levels:
            part = _dot_nt(ql[sl], kl[sl])
            if w < HG_CHUNK:
                part = jnp.where((ct & -w) == (cs & -w), part, 0.0)
            attn = attn + part
        o = _dot(attn.astype(BF16), v16[sl]) + diag[sl] * v[sl] + _dot_nt(q_in[sl], st.astype(BF16))
        outs.append(o)
        b_last = b[(ci + 1) * HG_CHUNK - 1:(ci + 1) * HG_CHUNK, :]
        st = st * jnp.exp(b_last) + _dot_tn(v16[sl], k_out[sl])
    st_ref[...] = st

    o = jnp.concatenate(outs, axis=0)
    g = g_ref[...]
    o_ref[...] = (_rms(o, HEAD_DIM) * gn_ref[...] * (g * _sigmoid(g))).astype(o_ref.dtype)


def hgrn2(proj, lower_bounds, out_norm, layer, batch, seq, rows=2048):
    depth = lower_bounds.shape[0]
    assert seq % rows == 0 and rows % HG_CHUNK == 0
    nr = seq // rows
    kern = functools.partial(_hgrn2_kernel, rows=rows, layer=layer)

    def col(c):
        return pl.BlockSpec((rows, HEAD_DIM), lambda b, h, r: (b * nr + r, c * N_HEADS + h))

    return pl.pallas_call(
        kern,
        grid=(batch, N_HEADS, nr),
        in_specs=[pl.BlockSpec((depth, HEAD_DIM), lambda b, h, r: (0, h)),
                  col(0), col(1), col(2), col(3),
                  pl.BlockSpec((1, HEAD_DIM), lambda b, h, r: (0, 0))],
        out_specs=pl.BlockSpec((rows, HEAD_DIM), lambda b, h, r: (b * nr + r, h)),
        out_shape=jax.ShapeDtypeStruct((batch * seq, N_HEADS * HEAD_DIM), BF16),
        scratch_shapes=[pltpu.VMEM((HEAD_DIM, HEAD_DIM), F32)],
        compiler_params=_params("parallel", "parallel", "arbitrary"),
        name="hgrn2",
    )(lower_bounds, proj, proj, proj, proj, out_norm.reshape(1, HEAD_DIM))


def _merge_kernel(a0, a1, a2, w0, w1, w2, g0, g1, g2, o_ref, wbf_ref):
    @pl.when(pl.program_id(1) == 0)
    def _():
        for n, w_ref in enumerate((w0, w1, w2)):
            _stage_weight(wbf_ref.at[n], w_ref)

    acc = _sigmoid(g0[...].astype(F32)) * _dot(a0[...], wbf_ref[0])
    acc = acc + _sigmoid(g1[...].astype(F32)) * _dot(a1[...], wbf_ref[1])
    acc = acc + _sigmoid(g2[...].astype(F32)) * _dot(a2[...], wbf_ref[2])
    o_ref[...] = acc.astype(o_ref.dtype)


def gated_merge(o_sb, o_mla, o_hg, w_sb, w_mla, w_hg, layer, gate_logits, tm=1024, tn=512):
    m, k = o_sb.shape
    n = w_sb.shape[2]
    nb = n // tn
    a_spec = pl.BlockSpec((tm, k), lambda j, i: (i, 0))
    w_spec = pl.BlockSpec((None, k, tn), lambda j, i: (layer, 0, j))

    def g_spec(branch):
        return pl.BlockSpec((tm, tn), lambda j, i: (i, branch * nb + j))

    return pl.pallas_call(
        _merge_kernel,
        grid=(nb, m // tm),
        in_specs=[a_spec, a_spec, a_spec, w_spec, w_spec, w_spec, g_spec(0), g_spec(1), g_spec(2)],
        out_specs=pl.BlockSpec((tm, tn), lambda j, i: (i, j)),
        out_shape=jax.ShapeDtypeStruct((m, n), BF16),
        scratch_shapes=[pltpu.VMEM((3, k, tn), BF16)],
        compiler_params=_params("arbitrary", "arbitrary"),
        name="gated_merge",
    )(o_sb, o_mla, o_hg, w_sb, w_mla, w_hg, gate_logits, gate_logits, gate_logits)


HALO = 16

def _up_conv_glu_kernel(a_ref, ah_ref, wg_ref, wv_ref, cg_ref, cv_ref, o_ref, wbf_ref, *, tiles_per_seq):
    @pl.when(pl.program_id(1) == 0)
    def _():
        _stage_weight(wbf_ref.at[0], wg_ref)
        _stage_weight(wbf_ref.at[1], wv_ref)

    a = a_ref[...]
    ah = ah_ref[...]
    seq_start = (pl.program_id(1) % tiles_per_seq) == 0
    row = lax.broadcasted_iota(jnp.int32, o_ref.shape, 0)

    def conv(slot, c_ref):
        w = wbf_ref[slot]
        u = _dot(a, w)
        halo = jnp.where(seq_start, 0.0, _dot(ah, w))
        prev1 = halo[HALO - 1:HALO]
        prev2 = halo[HALO - 2:HALO - 1]
        u1 = jnp.where(row == 0, prev1, pltpu.roll(u, 1, 0))
        u2 = jnp.where(row == 0, prev2, jnp.where(row == 1, prev1, pltpu.roll(u, 2, 0)))
        c = c_ref[...]
        return u2 * c[0:1] + u1 * c[1:2] + u * c[2:3]

    gate = conv(0, cg_ref)
    val = conv(1, cv_ref)
    o_ref[...] = (gate * _sigmoid(gate) * val).astype(o_ref.dtype)


def up_conv_glu(h, w_up, conv_w, layer, seq, tm=1024, tn=256):
    m, k = h.shape
    d_ff = w_up.shape[2] // 2
    nb = d_ff // tn
    kern = functools.partial(_up_conv_glu_kernel, tiles_per_seq=seq // tm)
    halo_blocks = tm // HALO
    return pl.pallas_call(
        kern,
        grid=(nb, m // tm),
        in_specs=[pl.BlockSpec((tm, k), lambda j, i: (i, 0)),
                  pl.BlockSpec((HALO, k), lambda j, i: (jnp.maximum(i * halo_blocks - 1, 0), 0)),
                  pl.BlockSpec((None, k, tn), lambda j, i: (layer, 0, j)),
                  pl.BlockSpec((None, k, tn), lambda j, i: (layer, 0, nb + j)),
                  pl.BlockSpec((None, conv_w.shape[1], tn), lambda j, i: (layer, 0, j)),
                  pl.BlockSpec((None, conv_w.shape[1], tn), lambda j, i: (layer, 0, nb + j))],
        out_specs=pl.BlockSpec((tm, tn), lambda j, i: (i, j)),
        out_shape=jax.ShapeDtypeStruct((m, d_ff), BF16),
        scratch_shapes=[pltpu.VMEM((2, k, tn), BF16)],
        compiler_params=_params("arbitrary", "arbitrary"),
        name="up_conv_glu",
    )(h, h, w_up, w_up, conv_w, conv_w)


def _rope_lane_tables(seq):
    pos = jnp.arange(seq, dtype=F32)
    inv = ROPE_THETA ** (-jnp.arange(0, MLA_ROPE, 2, dtype=F32) / MLA_ROPE)
    ang = pos[:, None] * inv[None, :]
    cos, sin = jnp.cos(ang), jnp.sin(ang)
    zeros = jnp.zeros((seq, HEAD_DIM - MLA_ROPE), F32)
    return (jnp.concatenate([cos, cos, zeros], axis=1),
            jnp.concatenate([-sin, sin, zeros], axis=1))


def _pad_head_gain(g):
    return jnp.pad(g, (0, MLA_PAD - MLA_QK)).reshape(1, MLA_PAD)


def kernel(x, lower_bounds, attn_norm, w_in, sb_q_norm, sb_k_norm, mla_q_a_norm, mla_kv_a_norm, mla_w_q_b, mla_w_kv_b, mla_q_norm, mla_k_norm, hg_out_norm, w_branch_sb, w_branch_mla, w_branch_hg, w_out, ffn_norm, w_up, ffn_conv, w_down):
    batch, seq, d_model = x.shape
    depth = w_in.shape[0]
    width = N_HEADS * HEAD_DIM
    q_lora = mla_w_q_b.shape[1]
    kv_lora = mla_w_kv_b.shape[1]
    c_attn = 3 * width + q_lora + kv_lora
    c_hg = c_attn + MLA_ROPE
    c_gate = c_hg + 4 * width

    cos_t, sin_t = _rope_lane_tables(seq)
    xf = x.reshape(batch * seq, d_model)
    w_in_t = jnp.swapaxes(w_in, 1, 2)
    for l in range(depth):
        wq = jnp.pad(mla_w_q_b[l], ((0, 0), (0, 0), (0, MLA_PAD - MLA_QK))).reshape(q_lora, N_HEADS * MLA_PAD)
        wkn = mla_w_kv_b[l][:, :, :HEAD_DIM].reshape(kv_lora, width)
        wv = mla_w_kv_b[l][:, :, HEAD_DIM:].reshape(kv_lora, width)

        h = rmsnorm_rows(xf, attn_norm[l])
        p_attn = weight_matmul(h, w_in_t, l, BF16, 1024, 512, transposed=True, n=c_attn)
        p_kr = weight_matmul(h, w_in_t, l, F32, 1024, HEAD_DIM, transposed=True, col0=c_attn, n=HEAD_DIM)
        p_hg = weight_matmul(h, w_in_t, l, F32, 1024, 512, transposed=True, col0=c_hg, n=c_gate - c_hg)
        p_gate = weight_matmul(h, w_in_t, l, BF16, 1024, 512, transposed=True, col0=c_gate)

        sb_qk = headnorm_qk(p_attn, jnp.stack([sb_q_norm[l], sb_k_norm[l]]).reshape(2, 1, HEAD_DIM), width)
        o_sb = sb_attention(sb_qk, p_attn, batch, seq)

        mq, mk, mv = mla_prep(p_attn, p_kr, mla_q_a_norm[l].reshape(1, q_lora),
                              mla_kv_a_norm[l].reshape(1, kv_lora), wq.astype(BF16), wkn.astype(BF16),
                              wv.astype(BF16), _pad_head_gain(mla_q_norm[l]), _pad_head_gain(mla_k_norm[l]),
                              cos_t, sin_t, seq, q_lora, kv_lora)
        o_mla = mla_attention(mq, mk, mv, batch, seq)

        o_hg = hgrn2(p_hg, lower_bounds, hg_out_norm[l], l, batch, seq)

        merged = gated_merge(o_sb, o_mla, o_hg, w_branch_sb, w_branch_mla, w_branch_hg, l, p_gate)
        xf = weight_matmul(merged, w_out, l, F32, 1024, 512, res=xf)

        h2 = rmsnorm_rows(xf, ffn_norm[l])
        act = up_conv_glu(h2, w_up, ffn_conv, l, seq)
        for part in range(2):
            xf = weight_matmul(act, w_down, l, F32, 1024, 512, k_part=(part, 2), res=xf)
    return xf.reshape(batch, seq, d_model)
```
